```python
import math
import jax, jax.numpy as jnp
from jax import lax
import numpy as np

D_MODEL = 1024
BATCH = 8
SEQ = 2048
DEPTH = 1
DEC_BATCH = 128
DEC_SEQ = 4
PAST_LEN = 2048
PAGE_SIZE = 128

W_RET = D_MODEL // 2
DK_RET = 128
DV_RET = DK_RET
H_RET = W_RET // DK_RET
W_SB = D_MODEL - W_RET
D_SB = 64
H_SB = W_SB // D_SB
MIX_WIDTH = W_RET + W_SB
IN_COLS = 4 * W_RET + 3 * W_SB
N_MEM = 256
H_X = 4
D_X = D_MODEL // H_X
D_FF = int(math.ceil(8 * D_MODEL / 3 / 128)) * 128
CHUNK = 128
Q_BLOCK = 128
ROPE_BASE = 10000.0
SB_BIAS_NEAR = 3.0
SB_BIAS_FAR = 8.0
EPS = 1e-6
F32 = jnp.float32

kernel_name = 'hymba_retention_stickbreak_macaron_step'


def rmsnorm(x, g):
    xf = x.astype(F32)
    y = xf * lax.rsqrt(jnp.mean(xf * xf, axis=-1, keepdims=True) + EPS) * g.astype(F32)
    return y.astype(x.dtype)


def swiglu(x, w_gate, w_up, w_down):
    return (jax.nn.silu(x @ w_gate) * (x @ w_up)) @ w_down


def macaron_ffn(x, g_pre, w_gate, w_up, w_down, g_post):
    return 0.5 * rmsnorm(swiglu(rmsnorm(x, g_pre), w_gate, w_up, w_down), g_post)


def rope(x, pos):
    d = x.shape[-1]
    inv = ROPE_BASE ** (-jnp.arange(0, d, 2, dtype=F32) / d)
    ang = pos.astype(F32)[:, None] * inv[None, :]
    cos = jnp.cos(ang)[None, :, None, :]
    sin = jnp.sin(ang)[None, :, None, :]
    x1, x2 = x[..., : d // 2], x[..., d // 2:]
    return jnp.concatenate([x1 * cos - x2 * sin, x1 * sin + x2 * cos], axis=-1)


def ret_log_decay():
    return jnp.log1p(-jnp.exp2(-5.0 - jnp.arange(H_RET, dtype=F32)))


def in_proj(xn, w_in, pos):
    b, l, _ = xn.shape
    p = (xn @ w_in).astype(F32)
    cuts = [W_RET, 2 * W_RET, 3 * W_RET, 4 * W_RET, 4 * W_RET + W_SB, 4 * W_RET + 2 * W_SB]
    q_r, k_r, v_r, g_r, q_s, k_s, v_s = jnp.split(p, cuts, axis=-1)
    q_r = rope(q_r.reshape(b, l, H_RET, DK_RET), pos)
    k_r = rope(k_r.reshape(b, l, H_RET, DK_RET), pos) * (DK_RET ** -0.5)
    v_r = v_r.reshape(b, l, H_RET, DV_RET)
    q_s = q_s.reshape(b, l, H_SB, D_SB)
    k_s = k_s.reshape(b, l, H_SB, D_SB)
    v_s = v_s.reshape(b, l, H_SB, D_SB)
    return q_r, k_r, v_r, g_r, q_s, k_s, v_s


def retention_chunk(q, k, v, s0):
    L = q.shape[1]
    lg = ret_log_decay()
    i = jnp.arange(L, dtype=F32)
    diff = i[:, None] - i[None, :]
    causal = diff >= 0
    dmat = jnp.where(causal[None], jnp.exp(jnp.where(causal, diff, 0.0)[None] * lg[:, None, None]), 0.0)
    scores = jnp.einsum('bihd,bjhd->bhij', q, k) * dmat[None]
    inner = jnp.einsum('bhij,bjhe->bihe', scores, v)
    cross = jnp.einsum('bihd,bhde->bihe', q, s0) * jnp.exp((i + 1.0)[:, None] * lg[None, :])[None, :, :, None]
    kd = k * jnp.exp((L - 1.0 - i)[:, None] * lg[None, :])[None, :, :, None]
    s_new = jnp.exp(L * lg)[None, :, None, None] * s0 + jnp.einsum('bjhd,bjhe->bhde', kd, v)
    return inner + cross, s_new


def retention_prompt(q, k, v):
    b, l = q.shape[0], q.shape[1]
    nc = l // CHUNK
    def to_chunks(t):
        return jnp.moveaxis(t.reshape(b, nc, CHUNK, t.shape[2], t.shape[3]), 1, 0)
    def step(s, c):
        o, s2 = retention_chunk(c[0], c[1], c[2], s)
        return s2, o
    s_init = jnp.zeros((b, H_RET, DK_RET, DV_RET), F32)
    s_fin, o = lax.scan(step, s_init, (to_chunks(q), to_chunks(k), to_chunks(v)))
    return jnp.moveaxis(o, 0, 1).reshape(b, l, H_RET, DV_RET), s_fin


def stick_breaking(q, k, v, bias):
    lq, lk = q.shape[1], k.shape[1]
    z = jnp.einsum('bqhd,bkhd->bhqk', q, k) * (D_SB ** -0.5) + bias.astype(F32)[None, :, None, None]
    t = (lk - lq) + jnp.arange(lq)
    s = jnp.arange(lk)
    mask = s[None, :] < t[:, None]
    log_1mb = jnp.where(mask, jax.nn.log_sigmoid(-z), 0.0)
    later = lax.cumsum(log_1mb, axis=3, reverse=True) - log_1mb
    a = jnp.where(mask, jnp.exp(jax.nn.log_sigmoid(z) + later), 0.0)
    return jnp.einsum('bhqk,bkhd->bqhd', a, v)


def stick_breaking_prompt(q, k, v, bias):
    l = q.shape[1]
    outs = [stick_breaking(q[:, i * Q_BLOCK:(i + 1) * Q_BLOCK], k[:, :(i + 1) * Q_BLOCK],
                           v[:, :(i + 1) * Q_BLOCK], bias)
            for i in range(l // Q_BLOCK)]
    return jnp.concatenate(outs, axis=1)


def mix_out(o_r, g_r, o_s, g_gn, w_out, dtype):
    b, l = o_r.shape[0], o_r.shape[1]
    mu = jnp.mean(o_r, axis=-1, keepdims=True)
    var = jnp.mean(jnp.square(o_r - mu), axis=-1, keepdims=True)
    o_r = (o_r - mu) * lax.rsqrt(var + EPS) * g_gn.astype(F32).reshape(H_RET, DV_RET)
    o_r = jax.nn.silu(g_r) * o_r.reshape(b, l, W_RET)
    cat = jnp.concatenate([o_r, o_s.reshape(b, l, W_SB)], axis=-1).astype(dtype)
    return cat @ w_out


def mem_kv(mem, g_mem, w_mk, w_mv):
    b, m, _ = mem.shape
    mn = rmsnorm(mem, g_mem)
    return (mn @ w_mk).reshape(b, m, H_X, D_X), (mn @ w_mv).reshape(b, m, H_X, D_X)


def cross_attn(xn, mk, mv, w_xq, w_xo):
    b, l, _ = xn.shape
    q = (xn @ w_xq).reshape(b, l, H_X, D_X).astype(F32)
    s = jnp.einsum('blhd,bmhd->bhlm', q, mk.astype(F32)) * (D_X ** -0.5)
    p = jax.nn.softmax(s, axis=-1)
    o = jnp.einsum('bhlm,bmhd->blhd', p, mv.astype(F32))
    return o.reshape(b, l, H_X * D_X).astype(xn.dtype) @ w_xo


def hybrid_layer(h, pos, mix_core, mk, mv, lw):
    (g_ffn1_pre, w_ffn1_gate, w_ffn1_up, w_ffn1_down, g_ffn1_post, g_mix_pre, w_in, g_ret_gn, w_out,
     g_mix_post, g_x_pre, w_xq, w_xo, g_x_post, g_ffn2_pre, w_ffn2_gate, w_ffn2_up, w_ffn2_down, g_ffn2_post) = lw
    h = h + macaron_ffn(h, g_ffn1_pre, w_ffn1_gate, w_ffn1_up, w_ffn1_down, g_ffn1_post)
    xn = rmsnorm(h, g_mix_pre)
    q_r, k_r, v_r, g_r, q_s, k_s, v_s = in_proj(xn, w_in, pos)
    o_r, o_s, s_new = mix_core(q_r, k_r, v_r, q_s, k_s, v_s)
    h = h + rmsnorm(mix_out(o_r, g_r, o_s, g_ret_gn, w_out, h.dtype), g_mix_post)
    h = h + rmsnorm(cross_attn(rmsnorm(h, g_x_pre), mk, mv, w_xq, w_xo), g_x_post)
    h = h + macaron_ffn(h, g_ffn2_pre, w_ffn2_gate, w_ffn2_up, w_ffn2_down, g_ffn2_post)
    return h, s_new, k_s, v_s


def setup_inputs(seed: int = 0) -> dict:
    key = jax.random.key(seed)
    ks = jax.random.split(key, 40)
    n_pages = PAST_LEN // PAGE_SIZE
    n_pool = (DEC_BATCH * n_pages * 5 + 3) // 4
    def nrm(k, shape, scale):
        return jax.random.normal(k, shape, F32) * scale
    def gain(k, n):
        return 1.0 + 0.05 * jax.random.normal(k, (DEPTH, n), F32)
    page_table = jax.random.permutation(ks[9], n_pool)[: DEC_BATCH * n_pages].reshape(DEC_BATCH, n_pages).astype(jnp.int32)
    dm = D_MODEL ** -0.5
    sb_bias = (-jnp.linspace(SB_BIAS_NEAR, SB_BIAS_FAR, H_SB, dtype=F32)[None, :]
               + 0.1 * jax.random.normal(ks[32], (DEPTH, H_SB), F32))
    return {
        'x_prompt': nrm(ks[0], (BATCH, SEQ, D_MODEL), 1.0),
        'x_sample': nrm(ks[1], (DEC_BATCH, DEC_SEQ, D_MODEL), 1.0),
        'mem_prompt': nrm(ks[2], (BATCH, N_MEM, D_MODEL), 1.0),
        'cache_ret_state': nrm(ks[3], (DEPTH, DEC_BATCH, H_RET, DK_RET, DV_RET), 0.5),
        'cache_sb_k': nrm(ks[4], (DEPTH, n_pool, PAGE_SIZE, H_SB, D_SB), 1.0),
        'cache_sb_v': nrm(ks[5], (DEPTH, n_pool, PAGE_SIZE, H_SB, D_SB), 1.0),
        'cache_mem_k': nrm(ks[6], (DEPTH, DEC_BATCH, N_MEM, H_X, D_X), 1.0),
        'cache_mem_v': nrm(ks[7], (DEPTH, DEC_BATCH, N_MEM, H_X, D_X), 1.0),
        'page_table': page_table,
        'g_ffn1_pre': gain(ks[10], D_MODEL),
        'w_ffn1_gate': nrm(ks[11], (DEPTH, D_MODEL, D_FF), dm),
        'w_ffn1_up': nrm(ks[12], (DEPTH, D_MODEL, D_FF), dm),
        'w_ffn1_down': nrm(ks[13], (DEPTH, D_FF, D_MODEL), D_FF ** -0.5),
        'g_ffn1_post': gain(ks[14], D_MODEL),
        'g_mix_pre': gain(ks[15], D_MODEL),
        'w_in': nrm(ks[16], (DEPTH, D_MODEL, IN_COLS), dm),
        'b_sb': sb_bias,
        'g_ret_gn': gain(ks[17], W_RET),
        'w_out': nrm(ks[18], (DEPTH, MIX_WIDTH, D_MODEL), MIX_WIDTH ** -0.5),
        'g_mix_post': gain(ks[19], D_MODEL),
        'g_mem': gain(ks[20], D_MODEL),
        'w_mk': nrm(ks[21], (DEPTH, D_MODEL, H_X * D_X), dm),
        'w_mv': nrm(ks[22], (DEPTH, D_MODEL, H_X * D_X), dm),
        'g_x_pre': gain(ks[23], D_MODEL),
        'w_xq': nrm(ks[24], (DEPTH, D_MODEL, H_X * D_X), dm),
        'w_xo': nrm(ks[25], (DEPTH, H_X * D_X, D_MODEL), (H_X * D_X) ** -0.5),
        'g_x_post': gain(ks[26], D_MODEL),
        'g_ffn2_pre': gain(ks[27], D_MODEL),
        'w_ffn2_gate': nrm(ks[28], (DEPTH, D_MODEL, D_FF), dm),
        'w_ffn2_up': nrm(ks[29], (DEPTH, D_MODEL, D_FF), dm),
        'w_ffn2_down': nrm(ks[30], (DEPTH, D_FF, D_MODEL), D_FF ** -0.5),
        'g_ffn2_post': gain(ks[31], D_MODEL),
    }


def reference(x_prompt, x_sample, mem_prompt, cache_ret_state, cache_sb_k, cache_sb_v, cache_mem_k, cache_mem_v,
              page_table, g_ffn1_pre, w_ffn1_gate, w_ffn1_up, w_ffn1_down, g_ffn1_post, g_mix_pre, w_in, b_sb,
              g_ret_gn, w_out, g_mix_post, g_mem, w_mk, w_mv, g_x_pre, w_xq, w_xo, g_x_post, g_ffn2_pre,
              w_ffn2_gate, w_ffn2_up, w_ffn2_down, g_ffn2_post):
    pos_p = jnp.arange(SEQ)
    pos_s = PAST_LEN + jnp.arange(x_sample.shape[1])
    hp, hs = x_prompt, x_sample
    rs_p, kp_l, vp_l, mk_l, mv_l, rs_s, ks_l, vs_l = [], [], [], [], [], [], [], []
    for l in range(DEPTH):
        lw = (g_ffn1_pre[l], w_ffn1_gate[l], w_ffn1_up[l], w_ffn1_down[l], g_ffn1_post[l], g_mix_pre[l], w_in[l],
              g_ret_gn[l], w_out[l], g_mix_post[l], g_x_pre[l], w_xq[l], w_xo[l], g_x_post[l], g_ffn2_pre[l],
              w_ffn2_gate[l], w_ffn2_up[l], w_ffn2_down[l], g_ffn2_post[l])
        bias_l = b_sb[l]

        def prompt_core(q_r, k_r, v_r, q_s, k_s, v_s):
            o_r, s_fin = retention_prompt(q_r, k_r, v_r)
            return o_r, stick_breaking_prompt(q_s, k_s, v_s, bias_l), s_fin
        mk_p, mv_p = mem_kv(mem_prompt, g_mem[l], w_mk[l], w_mv[l])
        hp, s_p, k_p, v_p = hybrid_layer(hp, pos_p, prompt_core, mk_p, mv_p, lw)

        s0 = cache_ret_state[l].astype(F32)
        kc, vc = cache_sb_k[l], cache_sb_v[l]
        def sample_core(q_r, k_r, v_r, q_s, k_s, v_s):
            o_r, s_new = retention_chunk(q_r, k_r, v_r, s0)
            b = q_s.shape[0]
            k_past = kc[page_table].reshape(b, -1, H_SB, D_SB).astype(F32)
            v_past = vc[page_table].reshape(b, -1, H_SB, D_SB).astype(F32)
            o_s = stick_breaking(q_s, jnp.concatenate([k_past, k_s], axis=1),
                                 jnp.concatenate([v_past, v_s], axis=1), bias_l)
            return o_r, o_s, s_new
        hs, s_s, k_n, v_n = hybrid_layer(hs, pos_s, sample_core, cache_mem_k[l], cache_mem_v[l], lw)

        dt = x_prompt.dtype
        rs_p.append(s_p.astype(dt)); kp_l.append(k_p.astype(dt)); vp_l.append(v_p.astype(dt))
        mk_l.append(mk_p.astype(dt)); mv_l.append(mv_p.astype(dt))
        rs_s.append(s_s.astype(dt)); ks_l.append(k_n.astype(dt)); vs_l.append(v_n.astype(dt))
    return (hp, hs, jnp.stack(rs_p), jnp.stack(kp_l), jnp.stack(vp_l), jnp.stack(mk_l), jnp.stack(mv_l),
            jnp.stack(rs_s), jnp.stack(ks_l), jnp.stack(vs_l))
```

```python
import functools

import jax
import jax.numpy as jnp
import numpy as np
from jax import lax
from jax.experimental import pallas as pl
from jax.experimental.pallas import tpu as pltpu

F32 = jnp.float32
BF16 = jnp.bfloat16

D_MODEL = 1024
BATCH = 8
SEQ = 2048
DEC_BATCH = 128
DEC_SEQ = 4
PAST_LEN = 2048
PAGE_SIZE = 128
N_PAGES = PAST_LEN // PAGE_SIZE
W_RET = 512
DK_RET = 128
H_RET = 4
W_SB = 512
D_SB = 64
H_SB = 8
IN_COLS = 4 * W_RET + 3 * W_SB
N_MEM = 256
H_X = 4
D_X = 256
D_FF = 2816
CHUNK = 128
ROPE_BASE = 10000.0
EPS = 1e-6

LANES = 128
ROW_TILE = 512
FF_TILE = 256
SB_TILE = 256
PAGES_PER_STEP = 8
RET_DEC_BLOCK = 8
XATT_DEC_BLOCK = 4
VMEM_LIMIT = 56 * 1024 * 1024

_NT = (((1,), (1,)), ((), ()))
_TN = (((0,), (0,)), ((), ()))


def _params(n_axes):
    return pltpu.CompilerParams(dimension_semantics=("arbitrary",) * n_axes, vmem_limit_bytes=VMEM_LIMIT)


def _const_spec(shape):
    zeros = (0,) * len(shape)
    return pl.BlockSpec(shape, lambda *_: zeros)


def _smem_spec():
    return pl.BlockSpec(memory_space=pltpu.SMEM)


def _rms(x, g):
    return x * lax.rsqrt(jnp.mean(x * x, axis=-1, keepdims=True) + EPS) * g


def _dot(a, b):
    return jnp.dot(a, b, preferred_element_type=F32)


def _silu(x):
    return x * jax.nn.sigmoid(x)


def _ffn_body(x_ref, gpre_ref, wg_ref, wu_ref, wd_ref, gpost_ref, o_ref, act_ref):
    x = x_ref[...]
    xn = _rms(x, gpre_ref[...]).astype(BF16)
    for c in range(D_FF // FF_TILE):
        sl = slice(c * FF_TILE, (c + 1) * FF_TILE)
        g = _dot(xn, wg_ref[:, sl])
        u = _dot(xn, wu_ref[:, sl])
        act_ref[:, sl] = (_silu(g) * u).astype(BF16)
    y = _dot(act_ref[...], wd_ref[...])
    o_ref[...] = x + 0.5 * _rms(y, gpost_ref[...])


def _ffn(x, g_pre, wg, wu, wd, g_post):
    m = x.shape[0]
    row = pl.BlockSpec((ROW_TILE, D_MODEL), lambda i: (i, 0))
    return pl.pallas_call(
        _ffn_body,
        grid=(m // ROW_TILE,),
        in_specs=[row, _const_spec((1, D_MODEL)), _const_spec((D_MODEL, D_FF)), _const_spec((D_MODEL, D_FF)),
                  _const_spec((D_FF, D_MODEL)), _const_spec((1, D_MODEL))],
        out_specs=row,
        out_shape=jax.ShapeDtypeStruct((m, D_MODEL), F32),
        scratch_shapes=[pltpu.VMEM((ROW_TILE, D_FF), BF16)],
        compiler_params=_params(1),
        name="ffn",
    )(x, g_pre, wg, wu, wd, g_post)


def _inproj_body(h_ref, g_ref, w_ref, cos_ref, sin_ref, qr_ref, kr_ref, vr_ref, gr_ref, qs_ref, ks_ref, vs_ref):
    xn = _rms(h_ref[...], g_ref[...]).astype(BF16)
    cos2 = cos_ref[...]
    sin2 = sin_ref[...]

    def proj(k):
        return _dot(xn, w_ref[:, k * W_RET:(k + 1) * W_RET])

    def rope(p):
        outs = []
        for h in range(H_RET):
            x = p[:, h * DK_RET:(h + 1) * DK_RET]
            outs.append(x * cos2 + pltpu.roll(x, DK_RET // 2, 1) * sin2)
        return jnp.concatenate(outs, axis=1)

    qr_ref[...] = rope(proj(0)).astype(qr_ref.dtype)
    kr_ref[...] = (rope(proj(1)) * (DK_RET ** -0.5)).astype(kr_ref.dtype)
    vr_ref[...] = proj(2).astype(vr_ref.dtype)
    gr_ref[...] = proj(3).astype(gr_ref.dtype)
    qs_ref[...] = (proj(4) * (D_SB ** -0.5)).astype(qs_ref.dtype)
    ks_ref[...] = proj(5).astype(ks_ref.dtype)
    vs_ref[...] = proj(6).astype(vs_ref.dtype)


def _inproj(h, g, w, cos2, sin2, dtypes):
    m = h.shape[0]
    n_tab = cos2.shape[0] // ROW_TILE
    row = pl.BlockSpec((ROW_TILE, D_MODEL), lambda i: (i, 0))
    tab = pl.BlockSpec((ROW_TILE, LANES), lambda i: (i % n_tab, 0))
    out = pl.BlockSpec((ROW_TILE, W_RET), lambda i: (i, 0))
    return pl.pallas_call(
        _inproj_body,
        grid=(m // ROW_TILE,),
        in_specs=[row, _const_spec((1, D_MODEL)), _const_spec((D_MODEL, IN_COLS)), tab, tab],
        out_specs=[out] * 7,
        out_shape=[jax.ShapeDtypeStruct((m, W_RET), dt) for dt in dtypes],
        compiler_params=_params(1),
        name="inproj",
    )(h, g, w, cos2, sin2)


def _ret_decays(lg, length):
    i = lax.broadcasted_iota(jnp.int32, (CHUNK, CHUNK), 0).astype(F32)
    j = lax.broadcasted_iota(jnp.int32, (CHUNK, CHUNK), 1).astype(F32)
    diff = i - j
    causal = diff >= 0
    dmat = jnp.where(causal, jnp.exp(jnp.where(causal, diff, 0.0) * lg), 0.0)
    rowdec = jnp.exp((i + 1.0) * lg)
    kdec = jnp.exp((length - 1.0 - i) * lg)
    sdec = jnp.exp(jnp.full((CHUNK, CHUNK), float(length), F32) * lg)
    return dmat, rowdec, kdec, sdec


def _ret_chunk(q, k, v, s, decays):
    dmat, rowdec, kdec, sdec = decays
    scores = lax.dot_general(q, k.astype(BF16), _NT, preferred_element_type=F32) * dmat
    inner = _dot(scores.astype(BF16), v)
    cross = _dot(q, s.astype(BF16)) * rowdec
    kd = (k * kdec).astype(BF16)
    s_new = sdec * s + lax.dot_general(kd, v, _TN, preferred_element_type=F32)
    return inner + cross, s_new


def _ret_gate(o, gr, ggn):
    mu = jnp.mean(o, axis=-1, keepdims=True)
    d = o - mu
    var = jnp.mean(d * d, axis=-1, keepdims=True)
    return _silu(gr) * (d * lax.rsqrt(var + EPS) * ggn)


def _ret_prompt_body(lg_ref, q_ref, k_ref, v_ref, gr_ref, ggn_ref, o_ref, s_ref):
    for h in range(H_RET):
        cols = slice(h * DK_RET, (h + 1) * DK_RET)
        decays = _ret_decays(lg_ref[h], CHUNK)
        ggn = ggn_ref[:, cols]

        def chunk(c, s, cols=cols, decays=decays, ggn=ggn):
            rows = pl.ds(pl.multiple_of(c * CHUNK, CHUNK), CHUNK)
            o, s_new = _ret_chunk(q_ref[rows, cols], k_ref[rows, cols], v_ref[rows, cols], s, decays)
            o_ref[rows, cols] = _ret_gate(o, gr_ref[rows, cols], ggn).astype(o_ref.dtype)
            return s_new

        s_ref[0, h] = lax.fori_loop(0, SEQ // CHUNK, chunk, jnp.zeros((CHUNK, CHUNK), F32))


def _ret_prompt(lg, q, k, v, gr, ggn):
    blk = pl.BlockSpec((SEQ, W_RET), lambda b: (b, 0))
    return pl.pallas_call(
        _ret_prompt_body,
        grid=(BATCH,),
        in_specs=[_smem_spec(), blk, blk, blk, blk, _const_spec((1, W_RET))],
        out_specs=[blk, pl.BlockSpec((1, H_RET, DK_RET, DK_RET), lambda b: (b, 0, 0, 0))],
        out_shape=[jax.ShapeDtypeStruct((BATCH * SEQ, W_RET), BF16),
                   jax.ShapeDtypeStruct((BATCH, H_RET, DK_RET, DK_RET), F32)],
        compiler_params=_params(1),
        name="ret_prompt",
    )(lg, q, k, v, gr, ggn)


def _ret_sample_body(lg_ref, q_ref, k_ref, v_ref, gr_ref, ggn_ref, s0_ref, o_ref, s_ref):
    pad = jnp.zeros((CHUNK - DEC_SEQ, DK_RET), F32)
    for h in range(H_RET):
        cols = slice(h * DK_RET, (h + 1) * DK_RET)
        decays = _ret_decays(lg_ref[h], DEC_SEQ)
        ggn = ggn_ref[:, cols]
        for b in range(RET_DEC_BLOCK):
            def tile(ref):
                return jnp.concatenate([ref[b, :, cols], pad], axis=0)
            k = tile(k_ref)
            o, s_new = _ret_chunk(tile(q_ref).astype(BF16), k, tile(v_ref).astype(BF16), s0_ref[b, h], decays)
            s_ref[b, h] = s_new
            o_ref[b, :, cols] = _ret_gate(o, tile(gr_ref), ggn)[:DEC_SEQ]


def _ret_sample(lg, q, k, v, gr, ggn, s0):
    tok = pl.BlockSpec((RET_DEC_BLOCK, DEC_SEQ, W_RET), lambda i: (i, 0, 0))
    st = pl.BlockSpec((RET_DEC_BLOCK, H_RET, DK_RET, DK_RET), lambda i: (i, 0, 0, 0))
    return pl.pallas_call(
        _ret_sample_body,
        grid=(DEC_BATCH // RET_DEC_BLOCK,),
        in_specs=[_smem_spec(), tok, tok, tok, tok, _const_spec((1, W_RET)), st],
        out_specs=[tok, st],
        out_shape=[jax.ShapeDtypeStruct((DEC_BATCH, DEC_SEQ, W_RET), F32),
                   jax.ShapeDtypeStruct((DEC_BATCH, H_RET, DK_RET, DK_RET), F32)],
        compiler_params=_params(1),
        name="ret_sample",
    )(lg, q, k, v, gr, ggn, s0)


def _cumsum_weights():
    s_src = np.arange(LANES)[:, None]
    s_dst = np.arange(LANES)[None, :]
    half = np.concatenate([(s_src > s_dst).astype(np.float32), np.ones((LANES, LANES), np.float32)], axis=1)
    return jnp.asarray(np.concatenate([half, half], axis=0), dtype=BF16)


def _log_one_minus_beta(z):
    return jnp.minimum(-z, 0.0) - jnp.log(1.0 + jnp.exp(-jnp.abs(z)))


def _split_bf16(x):
    hi = x.astype(BF16)
    lo = (x - hi.astype(F32)).astype(BF16)
    return hi, lo


def _sb_prompt_body(bias_ref, q_ref, k_ref, v_ref, w_ref, o_ref):
    hp = pl.program_id(1)
    t = SB_TILE
    lane = lax.broadcasted_iota(jnp.int32, (t, LANES), 1)
    row = lax.broadcasted_iota(jnp.int32, (t, t), 0)
    col = lax.broadcasted_iota(jnp.int32, (t, t), 1)
    strictly_earlier = col < row
    w = w_ref[...]

    def tile(qm, bias, kj, carry, diagonal):
        o, c = carry
        rows = pl.ds(pl.multiple_of(kj * t, t), t)
        kb = k_ref[rows, :].astype(BF16)
        vb = v_ref[rows, :].astype(BF16)
        z = lax.dot_general(qm, kb, _NT, preferred_element_type=F32) + bias
        lsm = _log_one_minus_beta(z)
        lsm_sum = jnp.where(strictly_earlier, lsm, 0.0) if diagonal else lsm
        hi, lo = _split_bf16(lsm_sum)
        r2 = _dot(jnp.concatenate([hi[:, LANES:], lo[:, LANES:]], axis=1), w)
        later2 = r2[:, :LANES] + c
        c = c + r2[:, LANES:]
        r1 = _dot(jnp.concatenate([hi[:, :LANES], lo[:, :LANES]], axis=1), w)
        later1 = r1[:, :LANES] + c
        c = c + r1[:, LANES:]
        a = jnp.exp(z + lsm + jnp.concatenate([later1, later2], axis=1))
        if diagonal:
            a = jnp.where(strictly_earlier, a, 0.0)
        return o + _dot(a.astype(BF16), vb), c

    def q_tile(qi, _):
        rows = pl.ds(pl.multiple_of(qi * t, t), t)
        qt = q_ref[rows, :]
        outs = []
        for hh in range(2):
            bias = bias_ref[2 * hp + hh]
            own = (lane < D_SB) if hh == 0 else (lane >= D_SB)
            qm = jnp.where(own, qt, jnp.zeros_like(qt))
            zero = jnp.zeros((t, LANES), F32)
            carry = tile(qm, bias, qi, (zero, zero), True)
            carry = lax.fori_loop(0, qi, lambda n, cr: tile(qm, bias, qi - 1 - n, cr, False), carry)
            outs.append(carry[0])
        o_ref[rows, :] = jnp.where(lane < D_SB, outs[0], outs[1]).astype(o_ref.dtype)
        return 0

    lax.fori_loop(0, SEQ // t, q_tile, 0)


def _sb_prompt(bias, q, k, v, w):
    blk = pl.BlockSpec((SEQ, LANES), lambda b, hp: (b, hp))
    return pl.pallas_call(
        _sb_prompt_body,
        grid=(BATCH, W_SB // LANES),
        in_specs=[_smem_spec(), blk, blk, blk, _const_spec((2 * LANES, 2 * LANES))],
        out_specs=blk,
        out_shape=jax.ShapeDtypeStruct((BATCH * SEQ, W_SB), BF16),
        compiler_params=_params(2),
        name="sb_prompt",
    )(bias, q, k, v, w)


def _sb_sample_body(pt_ref, bias_ref, q_ref, kn_ref, vn_ref, w_ref, *rest):
    kps = rest[:PAGES_PER_STEP]
    vps = rest[PAGES_PER_STEP:2 * PAGES_PER_STEP]
    o_ref, acc_ref, c_ref = rest[2 * PAGES_PER_STEP:]
    pj = pl.program_id(1)
    rows = H_SB * 8
    head_r = lax.broadcasted_iota(jnp.int32, (rows, W_SB), 0) // 8
    head_c = lax.broadcasted_iota(jnp.int32, (rows, W_SB), 1) // D_SB
    own = head_r == head_c
    q8 = jnp.concatenate([q_ref[0], jnp.zeros((8 - DEC_SEQ, W_SB), F32)], axis=0)
    qbd = jnp.where(own, jnp.concatenate([q8] * H_SB, axis=0), 0.0).astype(BF16)
    row128 = lax.broadcasted_iota(jnp.int32, (rows, LANES), 0)
    pos128 = lax.broadcasted_iota(jnp.int32, (rows, LANES), 1)
    bias = jnp.zeros((rows, LANES), F32)
    for h in range(H_SB):
        bias = jnp.where(row128 // 8 == h, bias_ref[h], bias)
    new_mask = pos128 < (row128 % 8)
    w = w_ref[...]

    def page(kt_ref, vt_ref, is_new):
        kt = kt_ref[0].reshape(W_SB, LANES).astype(BF16)
        vt = vt_ref[0].reshape(W_SB, LANES).astype(BF16)
        z = _dot(qbd, kt) + bias
        lsm = _log_one_minus_beta(z)
        lsm_sum = jnp.where(new_mask, lsm, 0.0) if is_new else lsm
        hi, lo = _split_bf16(lsm_sum)
        r = _dot(jnp.concatenate([hi, lo], axis=1), w)
        c = c_ref[...]
        a = jnp.exp(z + lsm + r[:, :LANES] + c)
        if is_new:
            a = jnp.where(new_mask, a, 0.0)
        c_ref[...] = c + r[:, LANES:]
        acc_ref[...] += lax.dot_general(a.astype(BF16), vt, _NT, preferred_element_type=F32)

    @pl.when(pj == 0)
    def _():
        acc_ref[...] = jnp.zeros_like(acc_ref)
        c_ref[...] = jnp.zeros_like(c_ref)
        page(kn_ref, vn_ref, True)

    for s in range(PAGES_PER_STEP):
        page(kps[s], vps[s], False)

    @pl.when(pj == pl.num_programs(1) - 1)
    def _():
        diag = jnp.where(own, acc_ref[...], 0.0).reshape(H_SB, 8, W_SB)
        o_ref[0] = jnp.sum(diag, axis=0)[:DEC_SEQ]


def _sb_sample(pt, bias, q, knew_t, vnew_t, w, kc_t, vc_t):
    steps = N_PAGES // PAGES_PER_STEP
    tok = pl.BlockSpec((1, DEC_SEQ, W_SB), lambda b, pj, pt: (b, 0, 0))
    new = pl.BlockSpec((1, H_SB, D_SB, PAGE_SIZE), lambda b, pj, pt: (b, 0, 0, 0))

    def page_spec(s):
        return pl.BlockSpec((1, H_SB, D_SB, PAGE_SIZE),
                            lambda b, pj, pt: (pt[b * N_PAGES + (N_PAGES - 1 - pj * PAGES_PER_STEP - s)], 0, 0, 0))

    pages = [page_spec(s) for s in range(PAGES_PER_STEP)]
    grid_spec = pltpu.PrefetchScalarGridSpec(
        num_scalar_prefetch=1,
        grid=(DEC_BATCH, steps),
        in_specs=[_smem_spec(), tok, new, new, pl.BlockSpec((2 * LANES, 2 * LANES), lambda b, pj, pt: (0, 0))]
        + pages + pages,
        out_specs=tok,
        scratch_shapes=[pltpu.VMEM((H_SB * 8, W_SB), F32), pltpu.VMEM((H_SB * 8, LANES), F32)],
    )
    return pl.pallas_call(
        _sb_sample_body,
        grid_spec=grid_spec,
        out_shape=jax.ShapeDtypeStruct((DEC_BATCH, DEC_SEQ, W_SB), F32),
        compiler_params=_params(2),
        name="sb_sample",
    )(pt, bias, q, knew_t, vnew_t, w, *([kc_t] * PAGES_PER_STEP), *([vc_t] * PAGES_PER_STEP))


def _mix_xq_body(h_ref, r_ref, s_ref, wout_ref, gpost_ref, gx_ref, wxq_ref, h2_ref, qx_ref):
    cat = jnp.concatenate([r_ref[...].astype(BF16), s_ref[...].astype(BF16)], axis=1)
    h2 = h_ref[...] + _rms(_dot(cat, wout_ref[...]), gpost_ref[...])
    h2_ref[...] = h2
    xn = _rms(h2, gx_ref[...]).astype(BF16)
    qx_ref[...] = (_dot(xn, wxq_ref[...]) * (D_X ** -0.5)).astype(qx_ref.dtype)


def _mix_xq(h, ret, sb, w_out, g_post, g_x, w_xq, q_dtype):
    m = h.shape[0]
    row = pl.BlockSpec((ROW_TILE, D_MODEL), lambda i: (i, 0))
    half = pl.BlockSpec((ROW_TILE, W_RET), lambda i: (i, 0))
    vec = _const_spec((1, D_MODEL))
    mat = _const_spec((D_MODEL, D_MODEL))
    return pl.pallas_call(
        _mix_xq_body,
        grid=(m // ROW_TILE,),
        in_specs=[row, half, half, mat, vec, vec, mat],
        out_specs=[row, row],
        out_shape=[jax.ShapeDtypeStruct((m, D_MODEL), F32), jax.ShapeDtypeStruct((m, D_MODEL), q_dtype)],
        compiler_params=_params(1),
        name="mix_xq",
    )(h, ret, sb, w_out, g_post, g_x, w_xq)


def _softmax_rows(s):
    e = jnp.exp(s - jnp.max(s, axis=-1, keepdims=True))
    return e / jnp.sum(e, axis=-1, keepdims=True)


def _xatt_prompt_body(q_ref, mk_ref, mv_ref, o_ref):
    for hx in range(H_X):
        cols = slice(hx * D_X, (hx + 1) * D_X)
        s = lax.dot_general(q_ref[:, cols], mk_ref[:, cols], _NT, preferred_element_type=F32)
        o_ref[:, cols] = _dot(_softmax_rows(s).astype(BF16), mv_ref[:, cols]).astype(o_ref.dtype)


def _xatt_prompt(q, mk, mv):
    m = q.shape[0]
    tiles_per_batch = SEQ // ROW_TILE
    row = pl.BlockSpec((ROW_TILE, D_MODEL), lambda i: (i, 0))
    mem = pl.BlockSpec((N_MEM, D_MODEL), lambda i: (i // tiles_per_batch, 0))
    return pl.pallas_call(
        _xatt_prompt_body,
        grid=(m // ROW_TILE,),
        in_specs=[row, mem, mem],
        out_specs=row,
        out_shape=jax.ShapeDtypeStruct((m, D_MODEL), BF16),
        compiler_params=_params(1),
        name="xatt_prompt",
    )(q, mk, mv)


def _xatt_sample_body(q_ref, mk_ref, mv_ref, o_ref):
    pad = jnp.zeros((8 - DEC_SEQ, D_X), F32)
    for b in range(XATT_DEC_BLOCK):
        for hx in range(H_X):
            cols = slice(hx * D_X, (hx + 1) * D_X)
            q8 = jnp.concatenate([q_ref[b, :, cols], pad], axis=0).astype(BF16)
            kh = mk_ref[b, :, hx, :].astype(BF16)
            vh = mv_ref[b, :, hx, :].astype(BF16)
            s = lax.dot_general(q8, kh, _NT, preferred_element_type=F32)
            o_ref[b, :, cols] = _dot(_softmax_rows(s).astype(BF16), vh)[:DEC_SEQ]


def _xatt_sample(q, mk, mv):
    tok = pl.BlockSpec((XATT_DEC_BLOCK, DEC_SEQ, D_MODEL), lambda i: (i, 0, 0))
    mem = pl.BlockSpec((XATT_DEC_BLOCK, N_MEM, H_X, D_X), lambda i: (i, 0, 0, 0))
    return pl.pallas_call(
        _xatt_sample_body,
        grid=(DEC_BATCH // XATT_DEC_BLOCK,),
        in_specs=[tok, mem, mem],
        out_specs=tok,
        out_shape=jax.ShapeDtypeStruct((DEC_BATCH, DEC_SEQ, D_MODEL), F32),
        compiler_params=_params(1),
        name="xatt_sample",
    )(q, mk, mv)


def _xo_body(h_ref, o_ref, wxo_ref, g_ref, out_ref):
    out_ref[...] = h_ref[...] + _rms(_dot(o_ref[...].astype(BF16), wxo_ref[...]), g_ref[...])


def _xo(h, o, w_xo, g):
    m = h.shape[0]
    row = pl.BlockSpec((ROW_TILE, D_MODEL), lambda i: (i, 0))
    return pl.pallas_call(
        _xo_body,
        grid=(m // ROW_TILE,),
        in_specs=[row, row, _const_spec((D_MODEL, D_MODEL)), _const_spec((1, D_MODEL))],
        out_specs=row,
        out_shape=jax.ShapeDtypeStruct((m, D_MODEL), F32),
        compiler_params=_params(1),
        name="xo",
    )(h, o, w_xo, g)


def _memkv_body(m_ref, g_ref, wk_ref, wv_ref, k_ref, v_ref, kb_ref, vb_ref):
    mn = _rms(m_ref[...], g_ref[...]).astype(BF16)
    k = _dot(mn, wk_ref[...])
    v = _dot(mn, wv_ref[...])
    k_ref[...] = k
    v_ref[...] = v
    kb_ref[...] = k.astype(BF16)
    vb_ref[...] = v.astype(BF16)


def _memkv(mem, g, w_mk, w_mv):
    m = mem.shape[0]
    row = pl.BlockSpec((ROW_TILE, D_MODEL), lambda i: (i, 0))
    mat = _const_spec((D_MODEL, D_MODEL))
    return pl.pallas_call(
        _memkv_body,
        grid=(m // ROW_TILE,),
        in_specs=[row, _const_spec((1, D_MODEL)), mat, mat],
        out_specs=[row] * 4,
        out_shape=[jax.ShapeDtypeStruct((m, D_MODEL), F32)] * 2 + [jax.ShapeDtypeStruct((m, D_MODEL), BF16)] * 2,
        compiler_params=_params(1),
        name="memkv",
    )(mem, g, w_mk, w_mv)


def _rope_tables(pos):
    inv = ROPE_BASE ** (-jnp.arange(0, DK_RET, 2, dtype=F32) / DK_RET)
    ang = pos.astype(F32)[:, None] * inv[None, :]
    cos, sin = jnp.cos(ang), jnp.sin(ang)
    return jnp.concatenate([cos, cos], axis=1), jnp.concatenate([-sin, sin], axis=1)


def kernel(x_prompt, x_sample, mem_prompt, cache_ret_state, cache_sb_k, cache_sb_v, cache_mem_k, cache_mem_v,
           page_table, g_ffn1_pre, w_ffn1_gate, w_ffn1_up, w_ffn1_down, g_ffn1_post, g_mix_pre, w_in, b_sb,
           g_ret_gn, w_out, g_mix_post, g_mem, w_mk, w_mv, g_x_pre, w_xq, w_xo, g_x_post, g_ffn2_pre,
           w_ffn2_gate, w_ffn2_up, w_ffn2_down, g_ffn2_post):
    l = 0

    def wt(w):
        return w[l].astype(BF16)

    (g_ffn1_pre, g_ffn1_post, g_mix_pre, g_ret_gn, g_mix_post, g_mem, g_x_pre, g_x_post, g_ffn2_pre,
     g_ffn2_post) = (g[l][None, :] for g in (g_ffn1_pre, g_ffn1_post, g_mix_pre, g_ret_gn, g_mix_post, g_mem,
                                             g_x_pre, g_x_post, g_ffn2_pre, g_ffn2_post))
    ffn1 = (g_ffn1_pre, wt(w_ffn1_gate), wt(w_ffn1_up), wt(w_ffn1_down), g_ffn1_post)
    ffn2 = (g_ffn2_pre, wt(w_ffn2_gate), wt(w_ffn2_up), wt(w_ffn2_down), g_ffn2_post)
    w_in_b, w_out_b, w_xq_b, w_xo_b = wt(w_in), wt(w_out), wt(w_xq), wt(w_xo)
    lg = jnp.log1p(-jnp.exp2(-5.0 - jnp.arange(H_RET, dtype=F32)))
    bias = b_sb[l]
    cumsum_w = _cumsum_weights()

    xp = x_prompt.reshape(BATCH * SEQ, D_MODEL)
    h = _ffn(xp, *ffn1)
    cos_p, sin_p = _rope_tables(jnp.arange(SEQ))
    q_r, k_r, v_r, g_r, q_s, k_s, v_s = _inproj(h, g_mix_pre, w_in_b, cos_p, sin_p,
                                                  (BF16, F32, BF16, F32, BF16, F32, F32))
    ret, rs_p = _ret_prompt(lg, q_r, k_r, v_r, g_r, g_ret_gn)
    o_s = _sb_prompt(bias, q_s, k_s, v_s, cumsum_w)
    mk_p, mv_p, mk_b, mv_b = _memkv(mem_prompt.reshape(BATCH * N_MEM, D_MODEL), g_mem, wt(w_mk), wt(w_mv))
    h, q_x = _mix_xq(h, ret, o_s, w_out_b, g_mix_post, g_x_pre, w_xq_b, BF16)
    h = _xo(h, _xatt_prompt(q_x, mk_b, mv_b), w_xo_b, g_x_post)
    y_prompt = _ffn(h, *ffn2).reshape(BATCH, SEQ, D_MODEL)

    n_tok = DEC_BATCH * DEC_SEQ
    hs = _ffn(x_sample.reshape(n_tok, D_MODEL), *ffn1)
    cos_s, sin_s = _rope_tables(PAST_LEN + jnp.arange(DEC_SEQ))
    cos_s, sin_s = jnp.tile(cos_s, (DEC_BATCH, 1)), jnp.tile(sin_s, (DEC_BATCH, 1))
    sq_r, sk_r, sv_r, sg_r, sq_s, sk_s, sv_s = _inproj(hs, g_mix_pre, w_in_b, cos_s, sin_s, (F32,) * 7)

    def per_seq(t):
        return t.reshape(DEC_BATCH, DEC_SEQ, t.shape[-1])

    s_ret, rs_s = _ret_sample(lg, per_seq(sq_r), per_seq(sk_r), per_seq(sv_r), per_seq(sg_r), g_ret_gn,
                              cache_ret_state[l])

    def new_page(t):
        t = t.reshape(DEC_BATCH, DEC_SEQ, H_SB, D_SB).transpose(0, 2, 3, 1)
        return jnp.pad(t, ((0, 0), (0, 0), (0, 0), (0, PAGE_SIZE - DEC_SEQ)))

    def pool_t(c):
        return c[l].transpose(0, 2, 3, 1)

    s_sb = _sb_sample(page_table.reshape(-1), bias, per_seq(sq_s), new_page(sk_s), new_page(sv_s), cumsum_w,
                      pool_t(cache_sb_k), pool_t(cache_sb_v))
    hs, sq_x = _mix_xq(hs, s_ret.reshape(n_tok, W_RET), s_sb.reshape(n_tok, W_SB), w_out_b, g_mix_post,
                       g_x_pre, w_xq_b, F32)
    so = _xatt_sample(per_seq(sq_x), cache_mem_k[l], cache_mem_v[l])
    hs = _xo(hs, so.reshape(n_tok, D_MODEL), w_xo_b, g_x_post)
    y_sample = _ffn(hs, *ffn2).reshape(DEC_BATCH, DEC_SEQ, D_MODEL)

    return (y_prompt, y_sample, rs_p[None],
            k_s.reshape(1, BATCH, SEQ, H_SB, D_SB), v_s.reshape(1, BATCH, SEQ, H_SB, D_SB),
            mk_p.reshape(1, BATCH, N_MEM, H_X, D_X), mv_p.reshape(1, BATCH, N_MEM, H_X, D_X),
            rs_s[None],
            sk_s.reshape(1, DEC_BATCH, DEC_SEQ, H_SB, D_SB), sv_s.reshape(1, DEC_BATCH, DEC_SEQ, H_SB, D_SB))
```

```python
import functools

import jax
import jax.numpy as jnp
import numpy as np
from jax import lax
from jax.experimental import pallas as pl
from jax.experimental.pallas import tpu as pltpu

F32 = jnp.float32
BF16 = jnp.bfloat16

D_MODEL = 1024
BATCH = 8
SEQ = 2048
DEC_BATCH = 128
DEC_SEQ = 4
PAST_LEN = 2048
PAGE_SIZE = 128
N_PAGES = PAST_LEN // PAGE_SIZE
W_RET = 512
DK_RET = 128
H_RET = 4
W_SB = 512
D_SB = 64
H_SB = 8
IN_COLS = 4 * W_RET + 3 * W_SB
N_MEM = 256
H_X = 4
D_X = 256
D_FF = 2816
CHUNK = 128
ROPE_BASE = 10000.0
EPS = 1e-6

LANES = 128
ROW_TILE = 512
FF_TILE = 256
SB_TILE = 256
RET_DEC_BLOCK = 8
XATT_DEC_BLOCK = 8
VMEM_LIMIT = 56 * 1024 * 1024

_NT = (((1,), (1,)), ((), ()))
_TN = (((0,), (0,)), ((), ()))


def _params(n_axes):
    return pltpu.CompilerParams(dimension_semantics=("arbitrary",) * n_axes, vmem_limit_bytes=VMEM_LIMIT)


def _const_spec(shape):
    zeros = (0,) * len(shape)
    return pl.BlockSpec(shape, lambda *_: zeros)


def _smem_spec():
    return pl.BlockSpec(memory_space=pltpu.SMEM)


def _rms(x, g):
    return x * lax.rsqrt(jnp.mean(x * x, axis=-1, keepdims=True) + EPS) * g


def _dot(a, b):
    return jnp.dot(a, b, preferred_element_type=F32)


def _silu(x):
    return x * jax.nn.sigmoid(x)


def _ffn_body(x_ref, gpre_ref, wg_ref, wu_ref, wd_ref, gpost_ref, o_ref, act_ref):
    x = x_ref[...]
    xn = _rms(x, gpre_ref[...]).astype(BF16)
    for c in range(D_FF // FF_TILE):
        sl = slice(c * FF_TILE, (c + 1) * FF_TILE)
        g = _dot(xn, wg_ref[:, sl])
        u = _dot(xn, wu_ref[:, sl])
        act_ref[:, sl] = (_silu(g) * u).astype(BF16)
    y = _dot(act_ref[...], wd_ref[...])
    o_ref[...] = x + 0.5 * _rms(y, gpost_ref[...])


def _ffn(x, g_pre, wg, wu, wd, g_post):
    m = x.shape[0]
    row = pl.BlockSpec((ROW_TILE, D_MODEL), lambda i: (i, 0))
    return pl.pallas_call(
        _ffn_body,
        grid=(m // ROW_TILE,),
        in_specs=[row, _const_spec((1, D_MODEL)), _const_spec((D_MODEL, D_FF)), _const_spec((D_MODEL, D_FF)),
                  _const_spec((D_FF, D_MODEL)), _const_spec((1, D_MODEL))],
        out_specs=row,
        out_shape=jax.ShapeDtypeStruct((m, D_MODEL), F32),
        scratch_shapes=[pltpu.VMEM((ROW_TILE, D_FF), BF16)],
        compiler_params=_params(1),
        name="ffn",
    )(x, g_pre, wg, wu, wd, g_post)


def _inproj_body(h_ref, g_ref, w_ref, cos_ref, sin_ref, qr_ref, kr_ref, vr_ref, gr_ref, qs_ref, ks_ref, vs_ref):
    xn = _rms(h_ref[...], g_ref[...]).astype(BF16)
    cos2 = cos_ref[...]
    sin2 = sin_ref[...]

    def proj(k):
        return _dot(xn, w_ref[:, k * W_RET:(k + 1) * W_RET])

    def rope(p):
        outs = []
        for h in range(H_RET):
            x = p[:, h * DK_RET:(h + 1) * DK_RET]
            outs.append(x * cos2 + pltpu.roll(x, DK_RET // 2, 1) * sin2)
        return jnp.concatenate(outs, axis=1)

    qr_ref[...] = rope(proj(0)).astype(qr_ref.dtype)
    kr_ref[...] = (rope(proj(1)) * (DK_RET ** -0.5)).astype(kr_ref.dtype)
    vr_ref[...] = proj(2).astype(vr_ref.dtype)
    gr_ref[...] = proj(3).astype(gr_ref.dtype)
    qs_ref[...] = (proj(4) * (D_SB ** -0.5)).astype(qs_ref.dtype)
    ks_ref[...] = proj(5).astype(ks_ref.dtype)
    vs_ref[...] = proj(6).astype(vs_ref.dtype)


def _inproj(h, g, w, cos2, sin2, dtypes):
    m = h.shape[0]
    n_tab = cos2.shape[0] // ROW_TILE
    row = pl.BlockSpec((ROW_TILE, D_MODEL), lambda i: (i, 0))
    tab = pl.BlockSpec((ROW_TILE, LANES), lambda i: (i % n_tab, 0))
    out = pl.BlockSpec((ROW_TILE, W_RET), lambda i: (i, 0))
    return pl.pallas_call(
        _inproj_body,
        grid=(m // ROW_TILE,),
        in_specs=[row, _const_spec((1, D_MODEL)), _const_spec((D_MODEL, IN_COLS)), tab, tab],
        out_specs=[out] * 7,
        out_shape=[jax.ShapeDtypeStruct((m, W_RET), dt) for dt in dtypes],
        compiler_params=_params(1),
        name="inproj",
    )(h, g, w, cos2, sin2)


def _ret_decays(lg, length):
    i = lax.broadcasted_iota(jnp.int32, (CHUNK, CHUNK), 0).astype(F32)
    j = lax.broadcasted_iota(jnp.int32, (CHUNK, CHUNK), 1).astype(F32)
    diff = i - j
    causal = diff >= 0
    dmat = jnp.where(causal, jnp.exp(jnp.where(causal, diff, 0.0) * lg), 0.0)
    rowdec = jnp.exp((i + 1.0) * lg)
    kdec = jnp.exp((length - 1.0 - i) * lg)
    sdec = jnp.exp(jnp.full((CHUNK, CHUNK), float(length), F32) * lg)
    return dmat, rowdec, kdec, sdec


def _ret_chunk_heads(qkv, states, decays):
    first = []
    for (q, k, v), s, (_, _, kdec, _) in zip(qkv, states, decays):
        scores = lax.dot_general(q, k.astype(BF16), _NT, preferred_element_type=F32)
        cross = _dot(q, s.astype(BF16))
        update = lax.dot_general((k * kdec).astype(BF16), v, _TN, preferred_element_type=F32)
        first.append((scores, cross, update))
    outs = []
    for (q, k, v), s, (dmat, rowdec, _, sdec), (scores, cross, update) in zip(qkv, states, decays, first):
        inner = _dot((scores * dmat).astype(BF16), v)
        outs.append((inner + cross * rowdec, sdec * s + update))
    return outs


def _ret_gate(o, gr, ggn):
    mu = jnp.mean(o, axis=-1, keepdims=True)
    d = o - mu
    var = jnp.mean(d * d, axis=-1, keepdims=True)
    return _silu(gr) * (d * lax.rsqrt(var + EPS) * ggn)


def _ret_prompt_body(lg_ref, q_ref, k_ref, v_ref, gr_ref, ggn_ref, o_ref, s_ref):
    heads = [slice(h * DK_RET, (h + 1) * DK_RET) for h in range(H_RET)]
    decays = [_ret_decays(lg_ref[h], CHUNK) for h in range(H_RET)]

    def chunk(c, states):
        rows = pl.ds(pl.multiple_of(c * CHUNK, CHUNK), CHUNK)
        qkv = [(q_ref[rows, cols], k_ref[rows, cols], v_ref[rows, cols]) for cols in heads]
        new_states = []
        for cols, (o, s_new) in zip(heads, _ret_chunk_heads(qkv, states, decays)):
            o_ref[rows, cols] = _ret_gate(o, gr_ref[rows, cols], ggn_ref[:, cols]).astype(o_ref.dtype)
            new_states.append(s_new)
        return tuple(new_states)

    zero = jnp.zeros((CHUNK, CHUNK), F32)
    states = lax.fori_loop(0, SEQ // CHUNK, chunk, (zero,) * H_RET)
    for h in range(H_RET):
        s_ref[0, h] = states[h]


def _ret_prompt(lg, q, k, v, gr, ggn):
    blk = pl.BlockSpec((SEQ, W_RET), lambda b: (b, 0))
    return pl.pallas_call(
        _ret_prompt_body,
        grid=(BATCH,),
        in_specs=[_smem_spec(), blk, blk, blk, blk, _const_spec((1, W_RET))],
        out_specs=[blk, pl.BlockSpec((1, H_RET, DK_RET, DK_RET), lambda b: (b, 0, 0, 0))],
        out_shape=[jax.ShapeDtypeStruct((BATCH * SEQ, W_RET), BF16),
                   jax.ShapeDtypeStruct((BATCH, H_RET, DK_RET, DK_RET), F32)],
        compiler_params=_params(1),
        name="ret_prompt",
    )(lg, q, k, v, gr, ggn)


def _ret_sample_body(lg_ref, q_ref, k_ref, v_ref, gr_ref, ggn_ref, s0_ref, o_ref, s_ref):
    pad = jnp.zeros((CHUNK - DEC_SEQ, DK_RET), F32)
    heads = [slice(h * DK_RET, (h + 1) * DK_RET) for h in range(H_RET)]
    decays = [_ret_decays(lg_ref[h], DEC_SEQ) for h in range(H_RET)]
    for b in range(RET_DEC_BLOCK):
        def tile(ref, cols):
            return jnp.concatenate([ref[b, :, cols], pad], axis=0)

        qkv = [(tile(q_ref, cols).astype(BF16), tile(k_ref, cols), tile(v_ref, cols).astype(BF16)) for cols in heads]
        states = [s0_ref[b, h] for h in range(H_RET)]
        for h, (o, s_new) in enumerate(_ret_chunk_heads(qkv, states, decays)):
            s_ref[b, h] = s_new
            o_ref[b, :, heads[h]] = _ret_gate(o, tile(gr_ref, heads[h]), ggn_ref[:, heads[h]])[:DEC_SEQ]


def _ret_sample(lg, q, k, v, gr, ggn, s0):
    tok = pl.BlockSpec((RET_DEC_BLOCK, DEC_SEQ, W_RET), lambda i: (i, 0, 0))
    st = pl.BlockSpec((RET_DEC_BLOCK, H_RET, DK_RET, DK_RET), lambda i: (i, 0, 0, 0))
    return pl.pallas_call(
        _ret_sample_body,
        grid=(DEC_BATCH // RET_DEC_BLOCK,),
        in_specs=[_smem_spec(), tok, tok, tok, tok, _const_spec((1, W_RET)), st],
        out_specs=[tok, st],
        out_shape=[jax.ShapeDtypeStruct((DEC_BATCH, DEC_SEQ, W_RET), F32),
                   jax.ShapeDtypeStruct((DEC_BATCH, H_RET, DK_RET, DK_RET), F32)],
        compiler_params=_params(1),
        name="ret_sample",
    )(lg, q, k, v, gr, ggn, s0)


def _cumsum_weights():
    s_src = np.arange(LANES)[:, None]
    s_dst = np.arange(LANES)[None, :]
    half = np.concatenate([(s_src > s_dst).astype(np.float32), np.ones((LANES, LANES), np.float32)], axis=1)
    return jnp.asarray(np.concatenate([half, half], axis=0), dtype=BF16)


def _log_one_minus_beta(z):
    return jnp.minimum(-z, 0.0) - jnp.log(1.0 + jnp.exp(-jnp.abs(z)))


def _split_bf16(x):
    hi = x.astype(BF16)
    lo = (x - hi.astype(F32)).astype(BF16)
    return hi, lo


def _sb_prompt_body(bias_ref, q_ref, k_ref, v_ref, w_ref, o_ref):
    hp = pl.program_id(1)
    t = SB_TILE
    lane = lax.broadcasted_iota(jnp.int32, (t, LANES), 1)
    row = lax.broadcasted_iota(jnp.int32, (t, t), 0)
    col = lax.broadcasted_iota(jnp.int32, (t, t), 1)
    strictly_earlier = col < row
    w = w_ref[...]

    def log_weights(z, diagonal):
        lsm = _log_one_minus_beta(z)
        hi, lo = _split_bf16(jnp.where(strictly_earlier, lsm, 0.0) if diagonal else lsm)
        halves = []
        for half in (slice(0, LANES), slice(LANES, 2 * LANES)):
            r = _dot(jnp.concatenate([hi[:, half], lo[:, half]], axis=1), w)
            halves.append((z[:, half] + lsm[:, half] + r[:, :LANES], r[:, LANES:]))
        return halves

    def attend(halves, vb, carry, diagonal):
        o, c = carry
        (logw1, total1), (logw2, total2) = halves
        a2 = jnp.exp(logw2 + c)
        c = c + total2
        a1 = jnp.exp(logw1 + c)
        c = c + total1
        a = jnp.concatenate([a1, a2], axis=1)
        if diagonal:
            a = jnp.where(strictly_earlier, a, 0.0)
        return o + _dot(a.astype(BF16), vb), c

    def q_tile(qi, _):
        rows = pl.ds(pl.multiple_of(qi * t, t), t)
        qt = q_ref[rows, :]
        zero_q = jnp.zeros_like(qt)
        qms = (jnp.where(lane < D_SB, qt, zero_q), jnp.where(lane >= D_SB, qt, zero_q))
        biases = (bias_ref[2 * hp], bias_ref[2 * hp + 1])

        def key_tiles(kjs, carries, diagonal=False):
            kvs = []
            for kj in kjs:
                krows = pl.ds(pl.multiple_of(kj * t, t), t)
                kvs.append((k_ref[krows, :].astype(BF16), v_ref[krows, :].astype(BF16)))
            zs = [[lax.dot_general(qms[hh], kb, _NT, preferred_element_type=F32) + biases[hh] for hh in range(2)]
                  for kb, _ in kvs]
            parts = [[log_weights(z, diagonal) for z in z2] for z2 in zs]
            carries = list(carries)
            for (_, vb), p2 in zip(kvs, parts):
                for hh in range(2):
                    carries[hh] = attend(p2[hh], vb, carries[hh], diagonal)
            return tuple(carries)

        def key_tile_pair(n, carries):
            kj = qi - 1 - (qi % 2) - 2 * n
            return key_tiles((kj, kj - 1), carries)

        zero = jnp.zeros((t, LANES), F32)
        carries = key_tiles((qi,), ((zero, zero), (zero, zero)), diagonal=True)
        carries = lax.cond(qi % 2 == 1, lambda cr: key_tiles((qi - 1,), cr), lambda cr: cr, carries)
        carries = lax.fori_loop(0, qi // 2, key_tile_pair, carries)
        o_ref[rows, :] = jnp.where(lane < D_SB, carries[0][0], carries[1][0]).astype(o_ref.dtype)
        return 0

    lax.fori_loop(0, SEQ // t, q_tile, 0)


def _sb_prompt(bias, q, k, v, w):
    blk = pl.BlockSpec((SEQ, LANES), lambda b, hp: (b, hp))
    return pl.pallas_call(
        _sb_prompt_body,
        grid=(BATCH, W_SB // LANES),
        in_specs=[_smem_spec(), blk, blk, blk, _const_spec((2 * LANES, 2 * LANES))],
        out_specs=blk,
        out_shape=jax.ShapeDtypeStruct((BATCH * SEQ, W_SB), BF16),
        compiler_params=_params(2),
        name="sb_prompt",
    )(bias, q, k, v, w)


def _sb_sample_body(pt_ref, bias_ref, q_ref, kn_ref, vn_ref, w_ref, *rest):
    kps = rest[:N_PAGES]
    vps = rest[N_PAGES:2 * N_PAGES]
    o_ref = rest[2 * N_PAGES]
    rows = H_SB * 8
    head_r = lax.broadcasted_iota(jnp.int32, (rows, W_SB), 0) // 8
    head_c = lax.broadcasted_iota(jnp.int32, (rows, W_SB), 1) // D_SB
    own = head_r == head_c
    q8 = jnp.concatenate([q_ref[0], jnp.zeros((8 - DEC_SEQ, W_SB), F32)], axis=0)
    qbd = jnp.where(own, jnp.concatenate([q8] * H_SB, axis=0), 0.0).astype(BF16)
    row128 = lax.broadcasted_iota(jnp.int32, (rows, LANES), 0)
    pos128 = lax.broadcasted_iota(jnp.int32, (rows, LANES), 1)
    bias = jnp.zeros((rows, LANES), F32)
    for h in range(H_SB):
        bias = jnp.where(row128 // 8 == h, bias_ref[h], bias)
    new_mask = pos128 < (row128 % 8)
    w = w_ref[...]

    def log_weights(z, mask):
        z = z + bias
        lsm = _log_one_minus_beta(z)
        hi, lo = _split_bf16(lsm if mask is None else jnp.where(mask, lsm, 0.0))
        r = _dot(jnp.concatenate([hi, lo], axis=1), w)
        return z + lsm + r[:, :LANES], r[:, LANES:]

    pad = jnp.zeros((PAGE_SIZE - DEC_SEQ, W_SB), F32)
    kn = jnp.concatenate([kn_ref[0], pad], axis=0).astype(BF16)
    vn = jnp.concatenate([vn_ref[0], pad], axis=0).astype(BF16)
    zs = [lax.dot_general(qbd, kn, _NT, preferred_element_type=F32)]
    zs += [_dot(qbd, kps[s][0].reshape(W_SB, PAGE_SIZE).astype(BF16)) for s in range(N_PAGES)]
    parts = [log_weights(z, new_mask if i == 0 else None) for i, z in enumerate(zs)]
    logw, c = parts[0]
    acc = _dot(jnp.where(new_mask, jnp.exp(logw), 0.0).astype(BF16), vn)
    for s in range(N_PAGES):
        logw, total = parts[s + 1]
        a = jnp.exp(logw + c)
        vt = vps[s][0].reshape(W_SB, PAGE_SIZE).astype(BF16)
        acc = acc + lax.dot_general(a.astype(BF16), vt, _NT, preferred_element_type=F32)
        c = c + total
    diag = jnp.where(own, acc, 0.0).reshape(H_SB, 8, W_SB)
    o_ref[0] = jnp.sum(diag, axis=0)[:DEC_SEQ]


def _sb_sample(pt, bias, q, k_new, v_new, w, kc_t, vc_t):
    tok = pl.BlockSpec((1, DEC_SEQ, W_SB), lambda b, pt: (b, 0, 0))

    def page_spec(s):
        return pl.BlockSpec((1, H_SB, D_SB, PAGE_SIZE), lambda b, pt: (pt[b * N_PAGES + (N_PAGES - 1 - s)], 0, 0, 0))

    pages = [page_spec(s) for s in range(N_PAGES)]
    grid_spec = pltpu.PrefetchScalarGridSpec(
        num_scalar_prefetch=1,
        grid=(DEC_BATCH,),
        in_specs=[_smem_spec(), tok, tok, tok, pl.BlockSpec((2 * LANES, 2 * LANES), lambda b, pt: (0, 0))]
        + pages + pages,
        out_specs=tok,
    )
    return pl.pallas_call(
        _sb_sample_body,
        grid_spec=grid_spec,
        out_shape=jax.ShapeDtypeStruct((DEC_BATCH, DEC_SEQ, W_SB), F32),
        compiler_params=_params(1),
        name="sb_sample",
    )(pt, bias, q, k_new, v_new, w, *([kc_t] * N_PAGES), *([vc_t] * N_PAGES))


def _mix_xq_body(h_ref, r_ref, s_ref, wout_ref, gpost_ref, gx_ref, wxq_ref, h2_ref, qx_ref):
    cat = jnp.concatenate([r_ref[...].astype(BF16), s_ref[...].astype(BF16)], axis=1)
    h2 = h_ref[...] + _rms(_dot(cat, wout_ref[...]), gpost_ref[...])
    h2_ref[...] = h2
    xn = _rms(h2, gx_ref[...]).astype(BF16)
    qx_ref[...] = (_dot(xn, wxq_ref[...]) * (D_X ** -0.5)).astype(qx_ref.dtype)


def _mix_xq(h, ret, sb, w_out, g_post, g_x, w_xq, q_dtype):
    m = h.shape[0]
    row = pl.BlockSpec((ROW_TILE, D_MODEL), lambda i: (i, 0))
    half = pl.BlockSpec((ROW_TILE, W_RET), lambda i: (i, 0))
    vec = _const_spec((1, D_MODEL))
    mat = _const_spec((D_MODEL, D_MODEL))
    return pl.pallas_call(
        _mix_xq_body,
        grid=(m // ROW_TILE,),
        in_specs=[row, half, half, mat, vec, vec, mat],
        out_specs=[row, row],
        out_shape=[jax.ShapeDtypeStruct((m, D_MODEL), F32), jax.ShapeDtypeStruct((m, D_MODEL), q_dtype)],
        compiler_params=_params(1),
        name="mix_xq",
    )(h, ret, sb, w_out, g_post, g_x, w_xq)


def _softmax_rows(s):
    e = jnp.exp(s - jnp.max(s, axis=-1, keepdims=True))
    return e / jnp.sum(e, axis=-1, keepdims=True)


def _xatt_prompt_body(q_ref, mk_ref, mv_ref, o_ref):
    for hx in range(H_X):
        cols = slice(hx * D_X, (hx + 1) * D_X)
        s = lax.dot_general(q_ref[:, cols], mk_ref[:, cols], _NT, preferred_element_type=F32)
        o_ref[:, cols] = _dot(_softmax_rows(s).astype(BF16), mv_ref[:, cols]).astype(o_ref.dtype)


def _xatt_prompt(q, mk, mv):
    m = q.shape[0]
    tiles_per_batch = SEQ // ROW_TILE
    row = pl.BlockSpec((ROW_TILE, D_MODEL), lambda i: (i, 0))
    mem = pl.BlockSpec((N_MEM, D_MODEL), lambda i: (i // tiles_per_batch, 0))
    return pl.pallas_call(
        _xatt_prompt_body,
        grid=(m // ROW_TILE,),
        in_specs=[row, mem, mem],
        out_specs=row,
        out_shape=jax.ShapeDtypeStruct((m, D_MODEL), BF16),
        compiler_params=_params(1),
        name="xatt_prompt",
    )(q, mk, mv)


def _xatt_sample_body(q_ref, mk_ref, mv_ref, o_ref):
    pad = jnp.zeros((8 - DEC_SEQ, D_X), F32)
    heads = [slice(hx * D_X, (hx + 1) * D_X) for hx in range(H_X)]
    pairs = [(b, cols) for b in range(XATT_DEC_BLOCK) for cols in heads]
    scores = []
    for b, cols in pairs:
        q8 = jnp.concatenate([q_ref[b, :, cols], pad], axis=0).astype(BF16)
        scores.append(lax.dot_general(q8, mk_ref[b, :, cols], _NT, preferred_element_type=F32))
    for (b, cols), s in zip(pairs, scores):
        o_ref[b, :, cols] = _dot(_softmax_rows(s).astype(BF16), mv_ref[b, :, cols])[:DEC_SEQ]


def _xatt_sample(q, mk, mv):
    tok = pl.BlockSpec((XATT_DEC_BLOCK, DEC_SEQ, D_MODEL), lambda i: (i, 0, 0))
    mem = pl.BlockSpec((XATT_DEC_BLOCK, N_MEM, D_MODEL), lambda i: (i, 0, 0))
    return pl.pallas_call(
        _xatt_sample_body,
        grid=(DEC_BATCH // XATT_DEC_BLOCK,),
        in_specs=[tok, mem, mem],
        out_specs=tok,
        out_shape=jax.ShapeDtypeStruct((DEC_BATCH, DEC_SEQ, D_MODEL), F32),
        compiler_params=_params(1),
        name="xatt_sample",
    )(q, mk, mv)


def _xo_body(h_ref, o_ref, wxo_ref, g_ref, out_ref):
    out_ref[...] = h_ref[...] + _rms(_dot(o_ref[...].astype(BF16), wxo_ref[...]), g_ref[...])


def _xo(h, o, w_xo, g):
    m = h.shape[0]
    row = pl.BlockSpec((ROW_TILE, D_MODEL), lambda i: (i, 0))
    return pl.pallas_call(
        _xo_body,
        grid=(m // ROW_TILE,),
        in_specs=[row, row, _const_spec((D_MODEL, D_MODEL)), _const_spec((1, D_MODEL))],
        out_specs=row,
        out_shape=jax.ShapeDtypeStruct((m, D_MODEL), F32),
        compiler_params=_params(1),
        name="xo",
    )(h, o, w_xo, g)


def _memkv_body(m_ref, g_ref, wk_ref, wv_ref, k_ref, v_ref, kb_ref, vb_ref):
    mn = _rms(m_ref[...], g_ref[...]).astype(BF16)
    k = _dot(mn, wk_ref[...])
    v = _dot(mn, wv_ref[...])
    k_ref[...] = k
    v_ref[...] = v
    kb_ref[...] = k.astype(BF16)
    vb_ref[...] = v.astype(BF16)


def _memkv(mem, g, w_mk, w_mv):
    m = mem.shape[0]
    row = pl.BlockSpec((ROW_TILE, D_MODEL), lambda i: (i, 0))
    mat = _const_spec((D_MODEL, D_MODEL))
    return pl.pallas_call(
        _memkv_body,
        grid=(m // ROW_TILE,),
        in_specs=[row, _const_spec((1, D_MODEL)), mat, mat],
        out_specs=[row] * 4,
        out_shape=[jax.ShapeDtypeStruct((m, D_MODEL), F32)] * 2 + [jax.ShapeDtypeStruct((m, D_MODEL), BF16)] * 2,
        compiler_params=_params(1),
        name="memkv",
    )(mem, g, w_mk, w_mv)


def _rope_tables(pos):
    inv = ROPE_BASE ** (-jnp.arange(0, DK_RET, 2, dtype=F32) / DK_RET)
    ang = pos.astype(F32)[:, None] * inv[None, :]
    cos, sin = jnp.cos(ang), jnp.sin(ang)
    return jnp.concatenate([cos, cos], axis=1), jnp.concatenate([-sin, sin], axis=1)


def kernel(x_prompt, x_sample, mem_prompt, cache_ret_state, cache_sb_k, cache_sb_v, cache_mem_k, cache_mem_v,
           page_table, g_ffn1_pre, w_ffn1_gate, w_ffn1_up, w_ffn1_down, g_ffn1_post, g_mix_pre, w_in, b_sb,
           g_ret_gn, w_out, g_mix_post, g_mem, w_mk, w_mv, g_x_pre, w_xq, w_xo, g_x_post, g_ffn2_pre,
           w_ffn2_gate, w_ffn2_up, w_ffn2_down, g_ffn2_post):
    l = 0

    def wt(w):
        return w[l].astype(BF16)

    (g_ffn1_pre, g_ffn1_post, g_mix_pre, g_ret_gn, g_mix_post, g_mem, g_x_pre, g_x_post, g_ffn2_pre,
     g_ffn2_post) = (g[l][None, :] for g in (g_ffn1_pre, g_ffn1_post, g_mix_pre, g_ret_gn, g_mix_post, g_mem,
                                             g_x_pre, g_x_post, g_ffn2_pre, g_ffn2_post))
    ffn1 = (g_ffn1_pre, wt(w_ffn1_gate), wt(w_ffn1_up), wt(w_ffn1_down), g_ffn1_post)
    ffn2 = (g_ffn2_pre, wt(w_ffn2_gate), wt(w_ffn2_up), wt(w_ffn2_down), g_ffn2_post)
    w_in_b, w_out_b, w_xq_b, w_xo_b = wt(w_in), wt(w_out), wt(w_xq), wt(w_xo)
    lg = jnp.log1p(-jnp.exp2(-5.0 - jnp.arange(H_RET, dtype=F32)))
    bias = b_sb[l]
    cumsum_w = _cumsum_weights()

    xp = x_prompt.reshape(BATCH * SEQ, D_MODEL)
    h = _ffn(xp, *ffn1)
    cos_p, sin_p = _rope_tables(jnp.arange(SEQ))
    q_r, k_r, v_r, g_r, q_s, k_s, v_s = _inproj(h, g_mix_pre, w_in_b, cos_p, sin_p,
                                                  (BF16, F32, BF16, F32, BF16, F32, F32))
    ret, rs_p = _ret_prompt(lg, q_r, k_r, v_r, g_r, g_ret_gn)
    o_s = _sb_prompt(bias, q_s, k_s, v_s, cumsum_w)
    mk_p, mv_p, mk_b, mv_b = _memkv(mem_prompt.reshape(BATCH * N_MEM, D_MODEL), g_mem, wt(w_mk), wt(w_mv))
    h, q_x = _mix_xq(h, ret, o_s, w_out_b, g_mix_post, g_x_pre, w_xq_b, BF16)
    h = _xo(h, _xatt_prompt(q_x, mk_b, mv_b), w_xo_b, g_x_post)
    y_prompt = _ffn(h, *ffn2).reshape(BATCH, SEQ, D_MODEL)

    n_tok = DEC_BATCH * DEC_SEQ
    hs = _ffn(x_sample.reshape(n_tok, D_MODEL), *ffn1)
    cos_s, sin_s = _rope_tables(PAST_LEN + jnp.arange(DEC_SEQ))
    cos_s, sin_s = jnp.tile(cos_s, (DEC_BATCH, 1)), jnp.tile(sin_s, (DEC_BATCH, 1))
    sq_r, sk_r, sv_r, sg_r, sq_s, sk_s, sv_s = _inproj(hs, g_mix_pre, w_in_b, cos_s, sin_s, (F32,) * 7)

    def per_seq(t):
        return t.reshape(DEC_BATCH, DEC_SEQ, t.shape[-1])

    s_ret, rs_s = _ret_sample(lg, per_seq(sq_r), per_seq(sk_r), per_seq(sv_r), per_seq(sg_r), g_ret_gn,
                              cache_ret_state[l])

    def pool_t(c):
        return c[l].transpose(0, 2, 3, 1)

    s_sb = _sb_sample(page_table.reshape(-1), bias, per_seq(sq_s), per_seq(sk_s), per_seq(sv_s), cumsum_w,
                      pool_t(cache_sb_k), pool_t(cache_sb_v))
    hs, sq_x = _mix_xq(hs, s_ret.reshape(n_tok, W_RET), s_sb.reshape(n_tok, W_SB), w_out_b, g_mix_post,
                       g_x_pre, w_xq_b, F32)
    def mem_rows(c):
        return c[l].reshape(DEC_BATCH, N_MEM, D_MODEL).astype(BF16)

    so = _xatt_sample(per_seq(sq_x), mem_rows(cache_mem_k), mem_rows(cache_mem_v))
    hs = _xo(hs, so.reshape(n_tok, D_MODEL), w_xo_b, g_x_post)
    y_sample = _ffn(hs, *ffn2).reshape(DEC_BATCH, DEC_SEQ, D_MODEL)

    return (y_prompt, y_sample, rs_p[None],
            k_s.reshape(1, BATCH, SEQ, H_SB, D_SB), v_s.reshape(1, BATCH, SEQ, H_SB, D_SB),
            mk_p.reshape(1, BATCH, N_MEM, H_X, D_X), mv_p.reshape(1, BATCH, N_MEM, H_X, D_X),
            rs_s[None],
            sk_s.reshape(1, DEC_BATCH, DEC_SEQ, H_SB, D_SB), sv_s.reshape(1, DEC_BATCH, DEC_SEQ, H_SB, D_SB))
```

```python
import functools

import jax
import jax.numpy as jnp
import numpy as np
from jax import lax
from jax.experimental import pallas as pl
from jax.experimental.pallas import tpu as pltpu

F32 = jnp.float32
BF16 = jnp.bfloat16

D_MODEL = 1024
BATCH = 8
SEQ = 2048
DEC_BATCH = 128
DEC_SEQ = 4
PAST_LEN = 2048
PAGE_SIZE = 128
N_PAGES = PAST_LEN // PAGE_SIZE
W_RET = 512
DK_RET = 128
H_RET = 4
W_SB = 512
D_SB = 64
H_SB = 8
IN_COLS = 4 * W_RET + 3 * W_SB
N_MEM = 256
H_X = 4
D_X = 256
D_FF = 2816
CHUNK = 128
ROPE_BASE = 10000.0
EPS = 1e-6
LOG2E = 1.4426950408889634

LANES = 128
ROW_TILE = 512
FF_TILE = 256
SB_TILE = 256
RET_DEC_BLOCK = 8
XATT_DEC_BLOCK = 4
VMEM_LIMIT = 56 * 1024 * 1024

_NT = (((1,), (1,)), ((), ()))
_TN = (((0,), (0,)), ((), ()))


def _params(n_axes):
    return pltpu.CompilerParams(dimension_semantics=("arbitrary",) * n_axes, vmem_limit_bytes=VMEM_LIMIT)


def _const_spec(shape):
    zeros = (0,) * len(shape)
    return pl.BlockSpec(shape, lambda *_: zeros)


def _smem_spec():
    return pl.BlockSpec(memory_space=pltpu.SMEM)


def _rms(x, g):
    return x * lax.rsqrt(jnp.mean(x * x, axis=-1, keepdims=True) + EPS) * g


def _dot(a, b):
    return jnp.dot(a, b, preferred_element_type=F32)


def _silu(x):
    return x * jax.nn.sigmoid(x)


def _ffn_body(x_ref, gpre_ref, wg_ref, wu_ref, wd_ref, gpost_ref, o_ref, act_ref):
    x = x_ref[...]
    xn = _rms(x, gpre_ref[...]).astype(BF16)
    for c in range(D_FF // FF_TILE):
        sl = slice(c * FF_TILE, (c + 1) * FF_TILE)
        g = _dot(xn, wg_ref[:, sl])
        u = _dot(xn, wu_ref[:, sl])
        act_ref[:, sl] = (_silu(g) * u).astype(BF16)
    y = _dot(act_ref[...], wd_ref[...])
    o_ref[...] = x + 0.5 * _rms(y, gpost_ref[...])


def _ffn(x, g_pre, wg, wu, wd, g_post):
    m = x.shape[0]
    row = pl.BlockSpec((ROW_TILE, D_MODEL), lambda i: (i, 0))
    return pl.pallas_call(
        _ffn_body,
        grid=(m // ROW_TILE,),
        in_specs=[row, _const_spec((1, D_MODEL)), _const_spec((D_MODEL, D_FF)), _const_spec((D_MODEL, D_FF)),
                  _const_spec((D_FF, D_MODEL)), _const_spec((1, D_MODEL))],
        out_specs=row,
        out_shape=jax.ShapeDtypeStruct((m, D_MODEL), F32),
        scratch_shapes=[pltpu.VMEM((ROW_TILE, D_FF), BF16)],
        compiler_params=_params(1),
        name="ffn",
    )(x, g_pre, wg, wu, wd, g_post)


def _inproj_body(h_ref, g_ref, w_ref, cos_ref, sin_ref, qr_ref, kr_ref, vr_ref, gr_ref, qs_ref, ks_ref, vs_ref):
    xn = _rms(h_ref[...], g_ref[...]).astype(BF16)
    cos2 = cos_ref[...]
    sin2 = sin_ref[...]

    def proj(k):
        return _dot(xn, w_ref[:, k * W_RET:(k + 1) * W_RET])

    def rope(p):
        outs = []
        for h in range(H_RET):
            x = p[:, h * DK_RET:(h + 1) * DK_RET]
            outs.append(x * cos2 + pltpu.roll(x, DK_RET // 2, 1) * sin2)
        return jnp.concatenate(outs, axis=1)

    qr_ref[...] = rope(proj(0)).astype(qr_ref.dtype)
    kr_ref[...] = (rope(proj(1)) * (DK_RET ** -0.5)).astype(kr_ref.dtype)
    vr_ref[...] = proj(2).astype(vr_ref.dtype)
    gr_ref[...] = proj(3).astype(gr_ref.dtype)
    qs_ref[...] = (proj(4) * (D_SB ** -0.5 * LOG2E)).astype(qs_ref.dtype)
    ks_ref[...] = proj(5).astype(ks_ref.dtype)
    vs_ref[...] = proj(6).astype(vs_ref.dtype)


def _inproj(h, g, w, cos2, sin2, dtypes):
    m = h.shape[0]
    n_tab = cos2.shape[0] // ROW_TILE
    row = pl.BlockSpec((ROW_TILE, D_MODEL), lambda i: (i, 0))
    tab = pl.BlockSpec((ROW_TILE, LANES), lambda i: (i % n_tab, 0))
    out = pl.BlockSpec((ROW_TILE, W_RET), lambda i: (i, 0))
    return pl.pallas_call(
        _inproj_body,
        grid=(m // ROW_TILE,),
        in_specs=[row, _const_spec((1, D_MODEL)), _const_spec((D_MODEL, IN_COLS)), tab, tab],
        out_specs=[out] * 7,
        out_shape=[jax.ShapeDtypeStruct((m, W_RET), dt) for dt in dtypes],
        compiler_params=_params(1),
        name="inproj",
    )(h, g, w, cos2, sin2)


def _ret_decays(lg, length):
    i = lax.broadcasted_iota(jnp.int32, (CHUNK, CHUNK), 0).astype(F32)
    j = lax.broadcasted_iota(jnp.int32, (CHUNK, CHUNK), 1).astype(F32)
    diff = i - j
    causal = diff >= 0
    dmat = jnp.where(causal, jnp.exp(jnp.where(causal, diff, 0.0) * lg), 0.0)
    rowdec = jnp.exp((i + 1.0) * lg)
    kdec = jnp.exp((length - 1.0 - i) * lg)
    sdec = jnp.exp(jnp.full((CHUNK, CHUNK), float(length), F32) * lg)
    return dmat, rowdec, kdec, sdec


def _ret_chunk_heads(qkv, states, decays):
    first = []
    for (q, k, v), s, (_, _, kdec, _) in zip(qkv, states, decays):
        scores = lax.dot_general(q, k.astype(BF16), _NT, preferred_element_type=F32)
        cross = _dot(q, s.astype(BF16))
        update = lax.dot_general((k * kdec).astype(BF16), v, _TN, preferred_element_type=F32)
        first.append((scores, cross, update))
    outs = []
    for (q, k, v), s, (dmat, rowdec, _, sdec), (scores, cross, update) in zip(qkv, states, decays, first):
        inner = _dot((scores * dmat).astype(BF16), v)
        outs.append((inner + cross * rowdec, sdec * s + update))
    return outs


def _ret_gate(o, gr, ggn):
    mu = jnp.mean(o, axis=-1, keepdims=True)
    d = o - mu
    var = jnp.mean(d * d, axis=-1, keepdims=True)
    return _silu(gr) * (d * lax.rsqrt(var + EPS) * ggn)


def _ret_prompt_body(lg_ref, q_ref, k_ref, v_ref, gr_ref, ggn_ref, o_ref, s_ref):
    heads = [slice(h * DK_RET, (h + 1) * DK_RET) for h in range(H_RET)]
    decays = [_ret_decays(lg_ref[h], CHUNK) for h in range(H_RET)]

    def chunk(c, states):
        rows = pl.ds(pl.multiple_of(c * CHUNK, CHUNK), CHUNK)
        qkv = [(q_ref[rows, cols], k_ref[rows, cols], v_ref[rows, cols]) for cols in heads]
        new_states = []
        for cols, (o, s_new) in zip(heads, _ret_chunk_heads(qkv, states, decays)):
            o_ref[rows, cols] = _ret_gate(o, gr_ref[rows, cols], ggn_ref[:, cols]).astype(o_ref.dtype)
            new_states.append(s_new)
        return tuple(new_states)

    zero = jnp.zeros((CHUNK, CHUNK), F32)
    states = lax.fori_loop(0, SEQ // CHUNK, chunk, (zero,) * H_RET)
    for h in range(H_RET):
        s_ref[0, h] = states[h]


def _ret_prompt(lg, q, k, v, gr, ggn):
    blk = pl.BlockSpec((SEQ, W_RET), lambda b: (b, 0))
    return pl.pallas_call(
        _ret_prompt_body,
        grid=(BATCH,),
        in_specs=[_smem_spec(), blk, blk, blk, blk, _const_spec((1, W_RET))],
        out_specs=[blk, pl.BlockSpec((1, H_RET, DK_RET, DK_RET), lambda b: (b, 0, 0, 0))],
        out_shape=[jax.ShapeDtypeStruct((BATCH * SEQ, W_RET), BF16),
                   jax.ShapeDtypeStruct((BATCH, H_RET, DK_RET, DK_RET), F32)],
        compiler_params=_params(1),
        name="ret_prompt",
    )(lg, q, k, v, gr, ggn)


def _ret_sample_body(lg_ref, q_ref, k_ref, v_ref, gr_ref, ggn_ref, s0_ref, o_ref, s_ref):
    pad = jnp.zeros((CHUNK - DEC_SEQ, DK_RET), F32)
    heads = [slice(h * DK_RET, (h + 1) * DK_RET) for h in range(H_RET)]
    decays = [_ret_decays(lg_ref[h], DEC_SEQ) for h in range(H_RET)]
    for b in range(RET_DEC_BLOCK):
        def tile(ref, cols):
            return jnp.concatenate([ref[b, :, cols], pad], axis=0)

        qkv = [(tile(q_ref, cols).astype(BF16), tile(k_ref, cols), tile(v_ref, cols).astype(BF16)) for cols in heads]
        states = [s0_ref[b, h] for h in range(H_RET)]
        for h, (o, s_new) in enumerate(_ret_chunk_heads(qkv, states, decays)):
            s_ref[b, h] = s_new
            o_ref[b, :, heads[h]] = _ret_gate(o, tile(gr_ref, heads[h]), ggn_ref[:, heads[h]])[:DEC_SEQ]


def _ret_sample(lg, q, k, v, gr, ggn, s0):
    tok = pl.BlockSpec((RET_DEC_BLOCK, DEC_SEQ, W_RET), lambda i: (i, 0, 0))
    st = pl.BlockSpec((RET_DEC_BLOCK, H_RET, DK_RET, DK_RET), lambda i: (i, 0, 0, 0))
    return pl.pallas_call(
        _ret_sample_body,
        grid=(DEC_BATCH // RET_DEC_BLOCK,),
        in_specs=[_smem_spec(), tok, tok, tok, tok, _const_spec((1, W_RET)), st],
        out_specs=[tok, st],
        out_shape=[jax.ShapeDtypeStruct((DEC_BATCH, DEC_SEQ, W_RET), F32),
                   jax.ShapeDtypeStruct((DEC_BATCH, H_RET, DK_RET, DK_RET), F32)],
        compiler_params=_params(1),
        name="ret_sample",
    )(lg, q, k, v, gr, ggn, s0)


def _cumsum_weights():
    s_src = np.arange(LANES)[:, None]
    s_dst = np.arange(LANES)[None, :]
    half = np.concatenate([(s_src > s_dst).astype(np.float32), np.ones((LANES, LANES), np.float32)], axis=1)
    return jnp.asarray(np.concatenate([half, half], axis=0), dtype=BF16)


def _softplus2(z2):
    return jnp.maximum(z2, 0.0) + jnp.log2(1.0 + jnp.exp2(jnp.minimum(z2, -z2)))


def _split_bf16(x):
    hi = x.astype(BF16)
    lo = (x - hi.astype(F32)).astype(BF16)
    return hi, lo


def _sb_prompt_body(bias_ref, q_ref, k_ref, v_ref, w_ref, o_ref):
    hp = pl.program_id(1)
    t = SB_TILE
    lane = lax.broadcasted_iota(jnp.int32, (t, LANES), 1)
    row = lax.broadcasted_iota(jnp.int32, (t, t), 0)
    col = lax.broadcasted_iota(jnp.int32, (t, t), 1)
    strictly_earlier = col < row
    w = w_ref[...]

    def log_weights(z, diagonal):
        sp = _softplus2(z)
        hi, lo = _split_bf16(jnp.where(strictly_earlier, sp, 0.0) if diagonal else sp)
        halves = []
        for half in (slice(0, LANES), slice(LANES, 2 * LANES)):
            r = _dot(jnp.concatenate([hi[:, half], lo[:, half]], axis=1), w)
            halves.append((z[:, half] - sp[:, half] - r[:, :LANES], r[:, LANES:]))
        return halves

    def attend(halves, vb, carry, diagonal):
        o, c = carry
        (logw1, total1), (logw2, total2) = halves
        a2 = jnp.exp2(logw2 - c)
        c = c + total2
        a1 = jnp.exp2(logw1 - c)
        c = c + total1
        a = jnp.concatenate([a1, a2], axis=1)
        if diagonal:
            a = jnp.where(strictly_earlier, a, 0.0)
        return o + _dot(a.astype(BF16), vb), c

    def q_tile(qi, _):
        rows = pl.ds(pl.multiple_of(qi * t, t), t)
        qt = q_ref[rows, :]
        zero_q = jnp.zeros_like(qt)
        qms = (jnp.where(lane < D_SB, qt, zero_q), jnp.where(lane >= D_SB, qt, zero_q))
        biases = (bias_ref[2 * hp] * LOG2E, bias_ref[2 * hp + 1] * LOG2E)

        def key_tiles(kjs, carries, diagonal=False):
            kvs = []
            for kj in kjs:
                krows = pl.ds(pl.multiple_of(kj * t, t), t)
                kvs.append((k_ref[krows, :].astype(BF16), v_ref[krows, :].astype(BF16)))
            zs = [[lax.dot_general(qms[hh], kb, _NT, preferred_element_type=F32) + biases[hh] for hh in range(2)]
                  for kb, _ in kvs]
            parts = [[log_weights(z, diagonal) for z in z2] for z2 in zs]
            carries = list(carries)
            for (_, vb), p2 in zip(kvs, parts):
                for hh in range(2):
                    carries[hh] = attend(p2[hh], vb, carries[hh], diagonal)
            return tuple(carries)

        def key_tile_pair(n, carries):
            kj = qi - 1 - (qi % 2) - 2 * n
            return key_tiles((kj, kj - 1), carries)

        zero = jnp.zeros((t, LANES), F32)
        carries = key_tiles((qi,), ((zero, zero), (zero, zero)), diagonal=True)
        carries = lax.cond(qi % 2 == 1, lambda cr: key_tiles((qi - 1,), cr), lambda cr: cr, carries)
        carries = lax.fori_loop(0, qi // 2, key_tile_pair, carries)
        o_ref[rows, :] = jnp.where(lane < D_SB, carries[0][0], carries[1][0]).astype(o_ref.dtype)
        return 0

    lax.fori_loop(0, SEQ // t, q_tile, 0)


def _sb_prompt(bias, q, k, v, w):
    blk = pl.BlockSpec((SEQ, LANES), lambda b, hp: (b, hp))
    return pl.pallas_call(
        _sb_prompt_body,
        grid=(BATCH, W_SB // LANES),
        in_specs=[_smem_spec(), blk, blk, blk, _const_spec((2 * LANES, 2 * LANES))],
        out_specs=blk,
        out_shape=jax.ShapeDtypeStruct((BATCH * SEQ, W_SB), BF16),
        compiler_params=_params(2),
        name="sb_prompt",
    )(bias, q, k, v, w)


def _sb_sample_body(pt_ref, bias_ref, q_ref, kn_ref, vn_ref, w_ref, *rest):
    kps = rest[:N_PAGES]
    vps = rest[N_PAGES:2 * N_PAGES]
    o_ref = rest[2 * N_PAGES]
    rows = H_SB * 8
    head_r = lax.broadcasted_iota(jnp.int32, (rows, W_SB), 0) // 8
    head_c = lax.broadcasted_iota(jnp.int32, (rows, W_SB), 1) // D_SB
    own = head_r == head_c
    q8 = jnp.concatenate([q_ref[0], jnp.zeros((8 - DEC_SEQ, W_SB), F32)], axis=0)
    qbd = jnp.where(own, jnp.concatenate([q8] * H_SB, axis=0), 0.0).astype(BF16)
    row128 = lax.broadcasted_iota(jnp.int32, (rows, LANES), 0)
    pos128 = lax.broadcasted_iota(jnp.int32, (rows, LANES), 1)
    bias = jnp.zeros((rows, LANES), F32)
    for h in range(H_SB):
        bias = jnp.where(row128 // 8 == h, bias_ref[h] * LOG2E, bias)
    new_mask = pos128 < (row128 % 8)
    w = w_ref[...]

    def log_weights(z, mask):
        z = z + bias
        sp = _softplus2(z)
        hi, lo = _split_bf16(sp if mask is None else jnp.where(mask, sp, 0.0))
        r = _dot(jnp.concatenate([hi, lo], axis=1), w)
        return z - sp - r[:, :LANES], r[:, LANES:]

    pad = jnp.zeros((PAGE_SIZE - DEC_SEQ, W_SB), F32)
    kn = jnp.concatenate([kn_ref[0], pad], axis=0).astype(BF16)
    vn = jnp.concatenate([vn_ref[0], pad], axis=0).astype(BF16)
    zs = [lax.dot_general(qbd, kn, _NT, preferred_element_type=F32)]
    zs += [_dot(qbd, kps[s][0].reshape(W_SB, PAGE_SIZE).astype(BF16)) for s in range(N_PAGES)]
    parts = [log_weights(z, new_mask if i == 0 else None) for i, z in enumerate(zs)]
    logw, c = parts[0]
    acc = _dot(jnp.where(new_mask, jnp.exp2(logw), 0.0).astype(BF16), vn)
    for s in range(N_PAGES):
        logw, total = parts[s + 1]
        a = jnp.exp2(logw - c)
        vt = vps[s][0].reshape(W_SB, PAGE_SIZE).astype(BF16)
        acc = acc + lax.dot_general(a.astype(BF16), vt, _NT, preferred_element_type=F32)
        c = c + total
    diag = jnp.where(own, acc, 0.0).reshape(H_SB, 8, W_SB)
    o_ref[0] = jnp.sum(diag, axis=0)[:DEC_SEQ]


def _sb_sample(pt, bias, q, k_new, v_new, w, kc_t, vc_t):
    tok = pl.BlockSpec((1, DEC_SEQ, W_SB), lambda b, pt: (b, 0, 0))

    def page_spec(s):
        return pl.BlockSpec((1, H_SB, D_SB, PAGE_SIZE), lambda b, pt: (pt[b * N_PAGES + (N_PAGES - 1 - s)], 0, 0, 0))

    pages = [page_spec(s) for s in range(N_PAGES)]
    grid_spec = pltpu.PrefetchScalarGridSpec(
        num_scalar_prefetch=1,
        grid=(DEC_BATCH,),
        in_specs=[_smem_spec(), tok, tok, tok, pl.BlockSpec((2 * LANES, 2 * LANES), lambda b, pt: (0, 0))]
        + pages + pages,
        out_specs=tok,
    )
    return pl.pallas_call(
        _sb_sample_body,
        grid_spec=grid_spec,
        out_shape=jax.ShapeDtypeStruct((DEC_BATCH, DEC_SEQ, W_SB), F32),
        compiler_params=_params(1),
        name="sb_sample",
    )(pt, bias, q, k_new, v_new, w, *([kc_t] * N_PAGES), *([vc_t] * N_PAGES))


def _mix_xq_body(h_ref, r_ref, s_ref, wout_ref, gpost_ref, gx_ref, wxq_ref, h2_ref, qx_ref):
    cat = jnp.concatenate([r_ref[...].astype(BF16), s_ref[...].astype(BF16)], axis=1)
    h2 = h_ref[...] + _rms(_dot(cat, wout_ref[...]), gpost_ref[...])
    h2_ref[...] = h2
    xn = _rms(h2, gx_ref[...]).astype(BF16)
    qx_ref[...] = (_dot(xn, wxq_ref[...]) * (D_X ** -0.5)).astype(qx_ref.dtype)


def _mix_xq(h, ret, sb, w_out, g_post, g_x, w_xq, q_dtype):
    m = h.shape[0]
    row = pl.BlockSpec((ROW_TILE, D_MODEL), lambda i: (i, 0))
    half = pl.BlockSpec((ROW_TILE, W_RET), lambda i: (i, 0))
    vec = _const_spec((1, D_MODEL))
    mat = _const_spec((D_MODEL, D_MODEL))
    return pl.pallas_call(
        _mix_xq_body,
        grid=(m // ROW_TILE,),
        in_specs=[row, half, half, mat, vec, vec, mat],
        out_specs=[row, row],
        out_shape=[jax.ShapeDtypeStruct((m, D_MODEL), F32), jax.ShapeDtypeStruct((m, D_MODEL), q_dtype)],
        compiler_params=_params(1),
        name="mix_xq",
    )(h, ret, sb, w_out, g_post, g_x, w_xq)


def _softmax_rows(s):
    e = jnp.exp(s - jnp.max(s, axis=-1, keepdims=True))
    return e / jnp.sum(e, axis=-1, keepdims=True)


def _xatt_prompt_body(q_ref, mk_ref, mv_ref, o_ref):
    for hx in range(H_X):
        cols = slice(hx * D_X, (hx + 1) * D_X)
        s = lax.dot_general(q_ref[:, cols], mk_ref[:, cols], _NT, preferred_element_type=F32)
        o_ref[:, cols] = _dot(_softmax_rows(s).astype(BF16), mv_ref[:, cols]).astype(o_ref.dtype)


def _xatt_prompt(q, mk, mv):
    m = q.shape[0]
    tiles_per_batch = SEQ // ROW_TILE
    row = pl.BlockSpec((ROW_TILE, D_MODEL), lambda i: (i, 0))
    mem = pl.BlockSpec((N_MEM, D_MODEL), lambda i: (i // tiles_per_batch, 0))
    return pl.pallas_call(
        _xatt_prompt_body,
        grid=(m // ROW_TILE,),
        in_specs=[row, mem, mem],
        out_specs=row,
        out_shape=jax.ShapeDtypeStruct((m, D_MODEL), BF16),
        compiler_params=_params(1),
        name="xatt_prompt",
    )(q, mk, mv)


def _xatt_sample_body(q_ref, mk_ref, mv_ref, o_ref):
    n_rows = 2 * H_X * 8
    n_keys = N_MEM * 8
    row = lax.broadcasted_iota(jnp.int32, (n_rows, n_keys), 0)
    col = lax.broadcasted_iota(jnp.int32, (n_rows, n_keys), 1)
    own = ((col // H_X) % 2 == row // (H_X * 8)) & (col % H_X == (row // 8) % H_X)
    row_h = lax.broadcasted_iota(jnp.int32, (H_X * 8, n_keys), 0)
    col_h = lax.broadcasted_iota(jnp.int32, (H_X * 8, n_keys), 1)
    summed = (col_h % 8) == H_X + row_h // 8
    pad = jnp.zeros((8 - DEC_SEQ, D_MODEL), F32)

    def rotate_lanes(x, shift):
        return jnp.concatenate([pltpu.roll(x[:, g * LANES:(g + 1) * LANES], shift, 1)
                                for g in range(x.shape[1] // LANES)], axis=1)

    def stacked_queries(b):
        q8 = jnp.concatenate([q_ref[b], pad], axis=0)
        blocks = [q8[:, (hx * 2 + half) * LANES:(hx * 2 + half + 1) * LANES] for half in range(2) for hx in range(H_X)]
        return jnp.concatenate(blocks, axis=0).astype(BF16)

    scores = [lax.dot_general(stacked_queries(b), mk_ref[b].reshape(n_keys, LANES).astype(BF16), _NT,
                              preferred_element_type=F32) for b in range(XATT_DEC_BLOCK)]
    for b, s in enumerate(scores):
        part = jnp.where(own, s, 0.0)
        part = part[:H_X * 8] + part[H_X * 8:]
        full = jnp.where(summed, part + rotate_lanes(part, H_X), -jnp.inf)
        p = _softmax_rows(full)
        p2 = jnp.concatenate([rotate_lanes(p, LANES - H_X), p], axis=0).astype(BF16)
        o = _dot(p2, mv_ref[b].reshape(n_keys, LANES).astype(BF16))
        blocks = [o[(half * H_X + hx) * 8:(half * H_X + hx + 1) * 8] for hx in range(H_X) for half in range(2)]
        o_ref[b] = jnp.concatenate(blocks, axis=1)[:DEC_SEQ]


def _xatt_sample(q, mk, mv):
    tok = pl.BlockSpec((XATT_DEC_BLOCK, DEC_SEQ, D_MODEL), lambda i: (i, 0, 0))
    mem = pl.BlockSpec((XATT_DEC_BLOCK, N_MEM, 8, LANES), lambda i: (i, 0, 0, 0))
    return pl.pallas_call(
        _xatt_sample_body,
        grid=(DEC_BATCH // XATT_DEC_BLOCK,),
        in_specs=[tok, mem, mem],
        out_specs=tok,
        out_shape=jax.ShapeDtypeStruct((DEC_BATCH, DEC_SEQ, D_MODEL), F32),
        compiler_params=_params(1),
        name="xatt_sample",
    )(q, mk, mv)


def _xo_body(h_ref, o_ref, wxo_ref, g_ref, out_ref):
    out_ref[...] = h_ref[...] + _rms(_dot(o_ref[...].astype(BF16), wxo_ref[...]), g_ref[...])


def _xo(h, o, w_xo, g):
    m = h.shape[0]
    row = pl.BlockSpec((ROW_TILE, D_MODEL), lambda i: (i, 0))
    return pl.pallas_call(
        _xo_body,
        grid=(m // ROW_TILE,),
        in_specs=[row, row, _const_spec((D_MODEL, D_MODEL)), _const_spec((1, D_MODEL))],
        out_specs=row,
        out_shape=jax.ShapeDtypeStruct((m, D_MODEL), F32),
        compiler_params=_params(1),
        name="xo",
    )(h, o, w_xo, g)


def _memkv_body(m_ref, g_ref, wk_ref, wv_ref, k_ref, v_ref, kb_ref, vb_ref):
    mn = _rms(m_ref[...], g_ref[...]).astype(BF16)
    k = _dot(mn, wk_ref[...])
    v = _dot(mn, wv_ref[...])
    k_ref[...] = k
    v_ref[...] = v
    kb_ref[...] = k.astype(BF16)
    vb_ref[...] = v.astype(BF16)


def _memkv(mem, g, w_mk, w_mv):
    m = mem.shape[0]
    row = pl.BlockSpec((ROW_TILE, D_MODEL), lambda i: (i, 0))
    mat = _const_spec((D_MODEL, D_MODEL))
    return pl.pallas_call(
        _memkv_body,
        grid=(m // ROW_TILE,),
        in_specs=[row, _const_spec((1, D_MODEL)), mat, mat],
        out_specs=[row] * 4,
        out_shape=[jax.ShapeDtypeStruct((m, D_MODEL), F32)] * 2 + [jax.ShapeDtypeStruct((m, D_MODEL), BF16)] * 2,
        compiler_params=_params(1),
        name="memkv",
    )(mem, g, w_mk, w_mv)


def _rope_tables(pos):
    inv = ROPE_BASE ** (-jnp.arange(0, DK_RET, 2, dtype=F32) / DK_RET)
    ang = pos.astype(F32)[:, None] * inv[None, :]
    cos, sin = jnp.cos(ang), jnp.sin(ang)
    return jnp.concatenate([cos, cos], axis=1), jnp.concatenate([-sin, sin], axis=1)


def kernel(x_prompt, x_sample, mem_prompt, cache_ret_state, cache_sb_k, cache_sb_v, cache_mem_k, cache_mem_v,
           page_table, g_ffn1_pre, w_ffn1_gate, w_ffn1_up, w_ffn1_down, g_ffn1_post, g_mix_pre, w_in, b_sb,
           g_ret_gn, w_out, g_mix_post, g_mem, w_mk, w_mv, g_x_pre, w_xq, w_xo, g_x_post, g_ffn2_pre,
           w_ffn2_gate, w_ffn2_up, w_ffn2_down, g_ffn2_post):
    l = 0

    def wt(w):
        return w[l].astype(BF16)

    (g_ffn1_pre, g_ffn1_post, g_mix_pre, g_ret_gn, g_mix_post, g_mem, g_x_pre, g_x_post, g_ffn2_pre,
     g_ffn2_post) = (g[l][None, :] for g in (g_ffn1_pre, g_ffn1_post, g_mix_pre, g_ret_gn, g_mix_post, g_mem,
                                             g_x_pre, g_x_post, g_ffn2_pre, g_ffn2_post))
    ffn1 = (g_ffn1_pre, wt(w_ffn1_gate), wt(w_ffn1_up), wt(w_ffn1_down), g_ffn1_post)
    ffn2 = (g_ffn2_pre, wt(w_ffn2_gate), wt(w_ffn2_up), wt(w_ffn2_down), g_ffn2_post)
    w_in_b, w_out_b, w_xq_b, w_xo_b = wt(w_in), wt(w_out), wt(w_xq), wt(w_xo)
    lg = jnp.log1p(-jnp.exp2(-5.0 - jnp.arange(H_RET, dtype=F32)))
    bias = b_sb[l]
    cumsum_w = _cumsum_weights()

    xp = x_prompt.reshape(BATCH * SEQ, D_MODEL)
    h = _ffn(xp, *ffn1)
    cos_p, sin_p = _rope_tables(jnp.arange(SEQ))
    q_r, k_r, v_r, g_r, q_s, k_s, v_s = _inproj(h, g_mix_pre, w_in_b, cos_p, sin_p,
                                                  (BF16, F32, BF16, F32, BF16, F32, F32))
    ret, rs_p = _ret_prompt(lg, q_r, k_r, v_r, g_r, g_ret_gn)
    o_s = _sb_prompt(bias, q_s, k_s, v_s, cumsum_w)
    mk_p, mv_p, mk_b, mv_b = _memkv(mem_prompt.reshape(BATCH * N_MEM, D_MODEL), g_mem, wt(w_mk), wt(w_mv))
    h, q_x = _mix_xq(h, ret, o_s, w_out_b, g_mix_post, g_x_pre, w_xq_b, BF16)
    h = _xo(h, _xatt_prompt(q_x, mk_b, mv_b), w_xo_b, g_x_post)
    y_prompt = _ffn(h, *ffn2).reshape(BATCH, SEQ, D_MODEL)

    n_tok = DEC_BATCH * DEC_SEQ
    hs = _ffn(x_sample.reshape(n_tok, D_MODEL), *ffn1)
    cos_s, sin_s = _rope_tables(PAST_LEN + jnp.arange(DEC_SEQ))
    cos_s, sin_s = jnp.tile(cos_s, (DEC_BATCH, 1)), jnp.tile(sin_s, (DEC_BATCH, 1))
    sq_r, sk_r, sv_r, sg_r, sq_s, sk_s, sv_s = _inproj(hs, g_mix_pre, w_in_b, cos_s, sin_s, (F32,) * 7)

    def per_seq(t):
        return t.reshape(DEC_BATCH, DEC_SEQ, t.shape[-1])

    s_ret, rs_s = _ret_sample(lg, per_seq(sq_r), per_seq(sk_r), per_seq(sv_r), per_seq(sg_r), g_ret_gn,
                              cache_ret_state[l])

    def pool_t(c):
        return c[l].transpose(0, 2, 3, 1)

    s_sb = _sb_sample(page_table.reshape(-1), bias, per_seq(sq_s), per_seq(sk_s), per_seq(sv_s), cumsum_w,
                      pool_t(cache_sb_k), pool_t(cache_sb_v))
    hs, sq_x = _mix_xq(hs, s_ret.reshape(n_tok, W_RET), s_sb.reshape(n_tok, W_SB), w_out_b, g_mix_post,
                       g_x_pre, w_xq_b, F32)
    def mem_rows(c):
        c = c[l].reshape(DEC_BATCH, N_MEM, H_X, 2, LANES)
        return c.transpose(0, 1, 3, 2, 4).reshape(DEC_BATCH, N_MEM, 2 * H_X, LANES)

    so = _xatt_sample(per_seq(sq_x), mem_rows(cache_mem_k), mem_rows(cache_mem_v))
    hs = _xo(hs, so.reshape(n_tok, D_MODEL), w_xo_b, g_x_post)
    y_sample = _ffn(hs, *ffn2).reshape(DEC_BATCH, DEC_SEQ, D_MODEL)

    return (y_prompt, y_sample, rs_p[None],
            k_s.reshape(1, BATCH, SEQ, H_SB, D_SB), v_s.reshape(1, BATCH, SEQ, H_SB, D_SB),
            mk_p.reshape(1, BATCH, N_MEM, H_X, D_X), mv_p.reshape(1, BATCH, N_MEM, H_X, D_X),
            rs_s[None],
            sk_s.reshape(1, DEC_BATCH, DEC_SEQ, H_SB, D_SB), sv_s.reshape(1, DEC_BATCH, DEC_SEQ, H_SB, D_SB))
```

```python
import functools

import jax
import jax.numpy as jnp
import numpy as np
from jax import lax
from jax.experimental import pallas as pl
from jax.experimental.pallas import tpu as pltpu

F32 = jnp.float32
BF16 = jnp.bfloat16

D_MODEL = 1024
BATCH = 8
SEQ = 2048
DEC_BATCH = 128
DEC_SEQ = 4
PAST_LEN = 2048
PAGE_SIZE = 128
N_PAGES = PAST_LEN // PAGE_SIZE
W_RET = 512
DK_RET = 128
H_RET = 4
W_SB = 512
D_SB = 64
H_SB = 8
IN_COLS = 4 * W_RET + 3 * W_SB
N_MEM = 256
H_X = 4
D_X = 256
D_FF = 2816
CHUNK = 128
ROPE_BASE = 10000.0
EPS = 1e-6
LOG2E = 1.4426950408889634

LANES = 128
ROW_TILE = 512
FF_TILE = 256
SB_TILE = 256
SB_HEADS_PER_STEP = 4
RET_DEC_BLOCK = 8
XATT_DEC_BLOCK = 4
VMEM_LIMIT = 56 * 1024 * 1024

_NT = (((1,), (1,)), ((), ()))
_TN = (((0,), (0,)), ((), ()))


def _params(n_axes):
    return pltpu.CompilerParams(dimension_semantics=("arbitrary",) * n_axes, vmem_limit_bytes=VMEM_LIMIT)


def _const_spec(shape):
    zeros = (0,) * len(shape)
    return pl.BlockSpec(shape, lambda *_: zeros)


def _smem_spec():
    return pl.BlockSpec(memory_space=pltpu.SMEM)


def _rms(x, g):
    return x * lax.rsqrt(jnp.mean(x * x, axis=-1, keepdims=True) + EPS) * g


def _dot(a, b):
    return jnp.dot(a, b, preferred_element_type=F32)


def _silu(x):
    return x * jax.nn.sigmoid(x)


def _ffn_body(x_ref, gpre_ref, wg_ref, wu_ref, wd_ref, gpost_ref, o_ref, act_ref):
    x = x_ref[...]
    xn = _rms(x, gpre_ref[...]).astype(BF16)
    for c in range(D_FF // FF_TILE):
        sl = slice(c * FF_TILE, (c + 1) * FF_TILE)
        g = _dot(xn, wg_ref[:, sl])
        u = _dot(xn, wu_ref[:, sl])
        act_ref[:, sl] = (_silu(g) * u).astype(BF16)
    y = _dot(act_ref[...], wd_ref[...])
    o_ref[...] = x + 0.5 * _rms(y, gpost_ref[...])


def _ffn(x, g_pre, wg, wu, wd, g_post):
    m = x.shape[0]
    row = pl.BlockSpec((ROW_TILE, D_MODEL), lambda i: (i, 0))
    return pl.pallas_call(
        _ffn_body,
        grid=(m // ROW_TILE,),
        in_specs=[row, _const_spec((1, D_MODEL)), _const_spec((D_MODEL, D_FF)), _const_spec((D_MODEL, D_FF)),
                  _const_spec((D_FF, D_MODEL)), _const_spec((1, D_MODEL))],
        out_specs=row,
        out_shape=jax.ShapeDtypeStruct((m, D_MODEL), F32),
        scratch_shapes=[pltpu.VMEM((ROW_TILE, D_FF), BF16)],
        compiler_params=_params(1),
        name="ffn",
    )(x, g_pre, wg, wu, wd, g_post)


def _inproj_body(h_ref, g_ref, w_ref, cos_ref, sin_ref, qr_ref, kr_ref, vr_ref, gr_ref, qs_ref, ks_ref, vs_ref,
                 *vsb_ref, transposed):
    xn = _rms(h_ref[...], g_ref[...]).astype(BF16)
    cos2 = cos_ref[...]
    sin2 = sin_ref[...]

    def proj(k):
        return _dot(xn, w_ref[:, k * W_RET:(k + 1) * W_RET])

    def rope(p):
        outs = []
        for h in range(H_RET):
            x = p[:, h * DK_RET:(h + 1) * DK_RET]
            outs.append(x * cos2 + pltpu.roll(x, DK_RET // 2, 1) * sin2)
        return jnp.concatenate(outs, axis=1)

    qr_ref[...] = rope(proj(0)).astype(qr_ref.dtype)
    kr_ref[...] = (rope(proj(1)) * (DK_RET ** -0.5)).astype(kr_ref.dtype)
    vr_ref[...] = proj(2).astype(vr_ref.dtype)
    gr_ref[...] = proj(3).astype(gr_ref.dtype)
    qs_ref[...] = (proj(4) * (D_SB ** -0.5 * LOG2E)).astype(qs_ref.dtype)
    ks, vs = proj(5), proj(6)
    if transposed:
        ks_ref[0] = ks.T
        vs_ref[0] = vs.T
        vsb_ref[0][...] = vs.astype(BF16)
    else:
        ks_ref[...] = ks
        vs_ref[...] = vs


def _inproj(h, g, w, cos2, sin2, dtypes, transposed):
    m = h.shape[0]
    n_tab = cos2.shape[0] // ROW_TILE
    row = pl.BlockSpec((ROW_TILE, D_MODEL), lambda i: (i, 0))
    tab = pl.BlockSpec((ROW_TILE, LANES), lambda i: (i % n_tab, 0))
    out = pl.BlockSpec((ROW_TILE, W_RET), lambda i: (i, 0))
    outs = [out] * 5
    shapes = [jax.ShapeDtypeStruct((m, W_RET), dt) for dt in dtypes]
    if transposed:
        tiles = SEQ // ROW_TILE
        out_t = pl.BlockSpec((1, W_SB, ROW_TILE), lambda i: (i // tiles, 0, i % tiles))
        outs += [out_t, out_t, out]
        shapes += [jax.ShapeDtypeStruct((m // SEQ, W_SB, SEQ), F32)] * 2 + [jax.ShapeDtypeStruct((m, W_SB), BF16)]
    else:
        outs += [out, out]
        shapes += [jax.ShapeDtypeStruct((m, W_SB), F32)] * 2
    return pl.pallas_call(
        functools.partial(_inproj_body, transposed=transposed),
        grid=(m // ROW_TILE,),
        in_specs=[row, _const_spec((1, D_MODEL)), _const_spec((D_MODEL, IN_COLS)), tab, tab],
        out_specs=outs,
        out_shape=shapes,
        compiler_params=_params(1),
        name="inproj",
    )(h, g, w, cos2, sin2)


def _ret_decays(lg, length):
    i = lax.broadcasted_iota(jnp.int32, (CHUNK, CHUNK), 0).astype(F32)
    j = lax.broadcasted_iota(jnp.int32, (CHUNK, CHUNK), 1).astype(F32)
    diff = i - j
    causal = diff >= 0
    dmat = jnp.where(causal, jnp.exp(jnp.where(causal, diff, 0.0) * lg), 0.0)
    rowdec = jnp.exp((i + 1.0) * lg)
    kdec = jnp.exp((length - 1.0 - i) * lg)
    sdec = jnp.exp(jnp.full((CHUNK, CHUNK), float(length), F32) * lg)
    return dmat, rowdec, kdec, sdec


def _ret_chunk_heads(qkv, states, decays):
    first = []
    for (q, k, v), s, (_, _, kdec, _) in zip(qkv, states, decays):
        scores = lax.dot_general(q, k.astype(BF16), _NT, preferred_element_type=F32)
        cross = _dot(q, s.astype(BF16))
        update = lax.dot_general((k * kdec).astype(BF16), v, _TN, preferred_element_type=F32)
        first.append((scores, cross, update))
    outs = []
    for (q, k, v), s, (dmat, rowdec, _, sdec), (scores, cross, update) in zip(qkv, states, decays, first):
        inner = _dot((scores * dmat).astype(BF16), v)
        outs.append((inner + cross * rowdec, sdec * s + update))
    return outs


def _ret_gate(o, gr, ggn):
    mu = jnp.mean(o, axis=-1, keepdims=True)
    d = o - mu
    var = jnp.mean(d * d, axis=-1, keepdims=True)
    return _silu(gr) * (d * lax.rsqrt(var + EPS) * ggn)


def _ret_prompt_body(lg_ref, q_ref, k_ref, v_ref, gr_ref, ggn_ref, o_ref, s_ref):
    heads = [slice(h * DK_RET, (h + 1) * DK_RET) for h in range(H_RET)]
    decays = [_ret_decays(lg_ref[h], CHUNK) for h in range(H_RET)]

    def chunk(c, states):
        rows = pl.ds(pl.multiple_of(c * CHUNK, CHUNK), CHUNK)
        qkv = [(q_ref[rows, cols], k_ref[rows, cols], v_ref[rows, cols]) for cols in heads]
        new_states = []
        for cols, (o, s_new) in zip(heads, _ret_chunk_heads(qkv, states, decays)):
            o_ref[rows, cols] = _ret_gate(o, gr_ref[rows, cols], ggn_ref[:, cols]).astype(o_ref.dtype)
            new_states.append(s_new)
        return tuple(new_states)

    zero = jnp.zeros((CHUNK, CHUNK), F32)
    states = lax.fori_loop(0, SEQ // CHUNK, chunk, (zero,) * H_RET)
    for h in range(H_RET):
        s_ref[0, h] = states[h]


def _ret_prompt(lg, q, k, v, gr, ggn):
    blk = pl.BlockSpec((SEQ, W_RET), lambda b: (b, 0))
    return pl.pallas_call(
        _ret_prompt_body,
        grid=(BATCH,),
        in_specs=[_smem_spec(), blk, blk, blk, blk, _const_spec((1, W_RET))],
        out_specs=[blk, pl.BlockSpec((1, H_RET, DK_RET, DK_RET), lambda b: (b, 0, 0, 0))],
        out_shape=[jax.ShapeDtypeStruct((BATCH * SEQ, W_RET), BF16),
                   jax.ShapeDtypeStruct((BATCH, H_RET, DK_RET, DK_RET), F32)],
        compiler_params=_params(1),
        name="ret_prompt",
    )(lg, q, k, v, gr, ggn)


def _ret_sample_body(lg_ref, q_ref, k_ref, v_ref, gr_ref, ggn_ref, s0_ref, o_ref, s_ref):
    pad = jnp.zeros((CHUNK - DEC_SEQ, DK_RET), F32)
    heads = [slice(h * DK_RET, (h + 1) * DK_RET) for h in range(H_RET)]
    decays = [_ret_decays(lg_ref[h], DEC_SEQ) for h in range(H_RET)]
    for b in range(RET_DEC_BLOCK):
        def tile(ref, cols):
            return jnp.concatenate([ref[b, :, cols], pad], axis=0)

        qkv = [(tile(q_ref, cols).astype(BF16), tile(k_ref, cols), tile(v_ref, cols).astype(BF16)) for cols in heads]
        states = [s0_ref[b, h] for h in range(H_RET)]
        for h, (o, s_new) in enumerate(_ret_chunk_heads(qkv, states, decays)):
            s_ref[b, h] = s_new
            o_ref[b, :, heads[h]] = _ret_gate(o, tile(gr_ref, heads[h]), ggn_ref[:, heads[h]])[:DEC_SEQ]


def _ret_sample(lg, q, k, v, gr, ggn, s0):
    tok = pl.BlockSpec((RET_DEC_BLOCK, DEC_SEQ, W_RET), lambda i: (i, 0, 0))
    st = pl.BlockSpec((RET_DEC_BLOCK, H_RET, DK_RET, DK_RET), lambda i: (i, 0, 0, 0))
    return pl.pallas_call(
        _ret_sample_body,
        grid=(DEC_BATCH // RET_DEC_BLOCK,),
        in_specs=[_smem_spec(), tok, tok, tok, tok, _const_spec((1, W_RET)), st],
        out_specs=[tok, st],
        out_shape=[jax.ShapeDtypeStruct((DEC_BATCH, DEC_SEQ, W_RET), F32),
                   jax.ShapeDtypeStruct((DEC_BATCH, H_RET, DK_RET, DK_RET), F32)],
        compiler_params=_params(1),
        name="ret_sample",
    )(lg, q, k, v, gr, ggn, s0)


def _cumsum_weights():
    s_src = np.arange(LANES)[:, None]
    s_dst = np.arange(LANES)[None, :]
    half = np.concatenate([(s_src > s_dst).astype(np.float32), np.ones((LANES, LANES), np.float32)], axis=1)
    return jnp.asarray(np.concatenate([half, half], axis=0), dtype=BF16)


def _softplus2(z2):
    return jnp.maximum(z2, 0.0) + jnp.log2(1.0 + jnp.exp2(jnp.minimum(z2, -z2)))


def _split_bf16(x):
    hi = x.astype(BF16)
    lo = (x - hi.astype(F32)).astype(BF16)
    return hi, lo


def _sb_prompt_body(bias_ref, q_ref, kt_ref, v_ref, w_ref, o_ref):
    hp = pl.program_id(1)
    t = SB_TILE
    lane = lax.broadcasted_iota(jnp.int32, (t, LANES), 1)
    row = lax.broadcasted_iota(jnp.int32, (t, t), 0)
    col = lax.broadcasted_iota(jnp.int32, (t, t), 1)
    strictly_earlier = col < row
    w = w_ref[...]

    def log_weights(z, diagonal):
        sp = _softplus2(z)
        hi, lo = _split_bf16(jnp.where(strictly_earlier, sp, 0.0) if diagonal else sp)
        halves = []
        for half in (slice(0, LANES), slice(LANES, 2 * LANES)):
            r = _dot(jnp.concatenate([hi[:, half], lo[:, half]], axis=1), w)
            halves.append((z[:, half] - sp[:, half] - r[:, :LANES], r[:, LANES:]))
        return halves

    def attend(halves, vb, carry, diagonal):
        o, c = carry
        (logw1, total1), (logw2, total2) = halves
        a2 = jnp.exp2(logw2 - c)
        c = c + total2
        a1 = jnp.exp2(logw1 - c)
        c = c + total1
        a = jnp.concatenate([a1, a2], axis=1)
        if diagonal:
            a = jnp.where(strictly_earlier, a, 0.0)
        return o + _dot(a.astype(BF16), vb), c

    def q_tile(qi, _):
        rows = pl.ds(pl.multiple_of(qi * t, t), t)
        qms, biases = [], []
        for hh in range(SB_HEADS_PER_STEP):
            qt = q_ref[rows, (hh // 2) * LANES:(hh // 2 + 1) * LANES]
            own = (lane < D_SB) if hh % 2 == 0 else (lane >= D_SB)
            qms.append(jnp.where(own, qt, jnp.zeros_like(qt)))
            biases.append(bias_ref[SB_HEADS_PER_STEP * hp + hh] * LOG2E)

        def key_tiles(kjs, carries, diagonal=False):
            kvs = []
            for kj in kjs:
                krows = pl.ds(pl.multiple_of(kj * t, t), t)
                kvs.append([(kt_ref[0, g * LANES:(g + 1) * LANES, krows].astype(BF16),
                             v_ref[krows, g * LANES:(g + 1) * LANES]) for g in range(SB_HEADS_PER_STEP // 2)])
            zs = [[_dot(qms[hh], kv[hh // 2][0]) + biases[hh] for hh in range(SB_HEADS_PER_STEP)] for kv in kvs]
            parts = [[log_weights(z, diagonal) for z in zh] for zh in zs]
            carries = list(carries)
            for kv, ph in zip(kvs, parts):
                for hh in range(SB_HEADS_PER_STEP):
                    carries[hh] = attend(ph[hh], kv[hh // 2][1], carries[hh], diagonal)
            return tuple(carries)

        zero = jnp.zeros((t, LANES), F32)
        carries = key_tiles((qi,), ((zero, zero),) * SB_HEADS_PER_STEP, diagonal=True)
        carries = lax.cond(qi % 2 == 1, lambda cr: key_tiles((qi - 1,), cr), lambda cr: cr, carries)
        top = qi - 1 - qi % 2

        def key_tile_pair(n, carries):
            return key_tiles((top - 2 * n, top - 2 * n - 1), carries)

        carries = lax.fori_loop(0, qi // 2, key_tile_pair, carries)
        for g in range(SB_HEADS_PER_STEP // 2):
            o_ref[rows, g * LANES:(g + 1) * LANES] = jnp.where(
                lane < D_SB, carries[2 * g][0], carries[2 * g + 1][0]).astype(o_ref.dtype)
        return 0

    lax.fori_loop(0, SEQ // t, q_tile, 0)


def _sb_prompt(bias, q, k_t, v, w):
    width = SB_HEADS_PER_STEP * D_SB
    blk = pl.BlockSpec((SEQ, width), lambda b, hp: (b, hp))
    blk_t = pl.BlockSpec((1, width, SEQ), lambda b, hp: (b, hp, 0))
    return pl.pallas_call(
        _sb_prompt_body,
        grid=(BATCH, W_SB // width),
        in_specs=[_smem_spec(), blk, blk_t, blk, _const_spec((2 * LANES, 2 * LANES))],
        out_specs=blk,
        out_shape=jax.ShapeDtypeStruct((BATCH * SEQ, W_SB), BF16),
        compiler_params=_params(2),
        name="sb_prompt",
    )(bias, q, k_t, v, w)


def _sb_sample_body(pt_ref, bias_ref, q_ref, kn_ref, vn_ref, w_ref, *rest):
    kps = rest[:N_PAGES]
    vps = rest[N_PAGES:2 * N_PAGES]
    o_ref = rest[2 * N_PAGES]
    rows = H_SB * 8
    head_r = lax.broadcasted_iota(jnp.int32, (rows, W_SB), 0) // 8
    head_c = lax.broadcasted_iota(jnp.int32, (rows, W_SB), 1) // D_SB
    own = head_r == head_c
    q8 = jnp.concatenate([q_ref[0], jnp.zeros((8 - DEC_SEQ, W_SB), F32)], axis=0)
    qbd = jnp.where(own, jnp.concatenate([q8] * H_SB, axis=0), 0.0).astype(BF16)
    row128 = lax.broadcasted_iota(jnp.int32, (rows, LANES), 0)
    pos128 = lax.broadcasted_iota(jnp.int32, (rows, LANES), 1)
    bias = jnp.zeros((rows, LANES), F32)
    for h in range(H_SB):
        bias = jnp.where(row128 // 8 == h, bias_ref[h] * LOG2E, bias)
    new_mask = pos128 < (row128 % 8)
    w = w_ref[...]

    def log_weights(z, mask):
        z = z + bias
        sp = _softplus2(z)
        hi, lo = _split_bf16(sp if mask is None else jnp.where(mask, sp, 0.0))
        r = _dot(jnp.concatenate([hi, lo], axis=1), w)
        return z - sp - r[:, :LANES], r[:, LANES:]

    pad = jnp.zeros((PAGE_SIZE - DEC_SEQ, W_SB), F32)
    kn = jnp.concatenate([kn_ref[0], pad], axis=0).astype(BF16)
    vn = jnp.concatenate([vn_ref[0], pad], axis=0).astype(BF16)
    zs = [lax.dot_general(qbd, kn, _NT, preferred_element_type=F32)]
    zs += [_dot(qbd, kps[s][0].reshape(W_SB, PAGE_SIZE).astype(BF16)) for s in range(N_PAGES)]
    parts = [log_weights(z, new_mask if i == 0 else None) for i, z in enumerate(zs)]
    logw, c = parts[0]
    acc = _dot(jnp.where(new_mask, jnp.exp2(logw), 0.0).astype(BF16), vn)
    for s in range(N_PAGES):
        logw, total = parts[s + 1]
        a = jnp.exp2(logw - c)
        vt = vps[s][0].reshape(W_SB, PAGE_SIZE).astype(BF16)
        acc = acc + lax.dot_general(a.astype(BF16), vt, _NT, preferred_element_type=F32)
        c = c + total
    diag = jnp.where(own, acc, 0.0).reshape(H_SB, 8, W_SB)
    o_ref[0] = jnp.sum(diag, axis=0)[:DEC_SEQ]


def _sb_sample(pt, bias, q, k_new, v_new, w, kc_t, vc_t):
    tok = pl.BlockSpec((1, DEC_SEQ, W_SB), lambda b, pt: (b, 0, 0))

    def page_spec(s):
        return pl.BlockSpec((1, H_SB, D_SB, PAGE_SIZE), lambda b, pt: (pt[b * N_PAGES + (N_PAGES - 1 - s)], 0, 0, 0))

    pages = [page_spec(s) for s in range(N_PAGES)]
    grid_spec = pltpu.PrefetchScalarGridSpec(
        num_scalar_prefetch=1,
        grid=(DEC_BATCH,),
        in_specs=[_smem_spec(), tok, tok, tok, pl.BlockSpec((2 * LANES, 2 * LANES), lambda b, pt: (0, 0))]
        + pages + pages,
        out_specs=tok,
    )
    return pl.pallas_call(
        _sb_sample_body,
        grid_spec=grid_spec,
        out_shape=jax.ShapeDtypeStruct((DEC_BATCH, DEC_SEQ, W_SB), F32),
        compiler_params=_params(1),
        name="sb_sample",
    )(pt, bias, q, k_new, v_new, w, *([kc_t] * N_PAGES), *([vc_t] * N_PAGES))


def _mix_xq_body(h_ref, r_ref, s_ref, wout_ref, gpost_ref, gx_ref, wxq_ref, h2_ref, qx_ref):
    cat = jnp.concatenate([r_ref[...].astype(BF16), s_ref[...].astype(BF16)], axis=1)
    h2 = h_ref[...] + _rms(_dot(cat, wout_ref[...]), gpost_ref[...])
    h2_ref[...] = h2
    xn = _rms(h2, gx_ref[...]).astype(BF16)
    qx_ref[...] = (_dot(xn, wxq_ref[...]) * (D_X ** -0.5)).astype(qx_ref.dtype)


def _mix_xq(h, ret, sb, w_out, g_post, g_x, w_xq, q_dtype):
    m = h.shape[0]
    row = pl.BlockSpec((ROW_TILE, D_MODEL), lambda i: (i, 0))
    half = pl.BlockSpec((ROW_TILE, W_RET), lambda i: (i, 0))
    vec = _const_spec((1, D_MODEL))
    mat = _const_spec((D_MODEL, D_MODEL))
    return pl.pallas_call(
        _mix_xq_body,
        grid=(m // ROW_TILE,),
        in_specs=[row, half, half, mat, vec, vec, mat],
        out_specs=[row, row],
        out_shape=[jax.ShapeDtypeStruct((m, D_MODEL), F32), jax.ShapeDtypeStruct((m, D_MODEL), q_dtype)],
        compiler_params=_params(1),
        name="mix_xq",
    )(h, ret, sb, w_out, g_post, g_x, w_xq)


def _softmax_rows(s):
    e = jnp.exp(s - jnp.max(s, axis=-1, keepdims=True))
    return e / jnp.sum(e, axis=-1, keepdims=True)


def _xatt_prompt_body(q_ref, mk_ref, mv_ref, o_ref):
    for hx in range(H_X):
        cols = slice(hx * D_X, (hx + 1) * D_X)
        s = lax.dot_general(q_ref[:, cols], mk_ref[:, cols], _NT, preferred_element_type=F32)
        o_ref[:, cols] = _dot(_softmax_rows(s).astype(BF16), mv_ref[:, cols]).astype(o_ref.dtype)


def _xatt_prompt(q, mk, mv):
    m = q.shape[0]
    tiles_per_batch = SEQ // ROW_TILE
    row = pl.BlockSpec((ROW_TILE, D_MODEL), lambda i: (i, 0))
    mem = pl.BlockSpec((N_MEM, D_MODEL), lambda i: (i // tiles_per_batch, 0))
    return pl.pallas_call(
        _xatt_prompt_body,
        grid=(m // ROW_TILE,),
        in_specs=[row, mem, mem],
        out_specs=row,
        out_shape=jax.ShapeDtypeStruct((m, D_MODEL), BF16),
        compiler_params=_params(1),
        name="xatt_prompt",
    )(q, mk, mv)


def _xatt_sample_body(q_ref, mk_ref, mv_ref, o_ref):
    n_rows = 2 * H_X * 8
    n_keys = N_MEM * 8
    row = lax.broadcasted_iota(jnp.int32, (n_rows, n_keys), 0)
    col = lax.broadcasted_iota(jnp.int32, (n_rows, n_keys), 1)
    own = ((col // H_X) % 2 == row // (H_X * 8)) & (col % H_X == (row // 8) % H_X)
    row_h = lax.broadcasted_iota(jnp.int32, (H_X * 8, n_keys), 0)
    col_h = lax.broadcasted_iota(jnp.int32, (H_X * 8, n_keys), 1)
    summed = (col_h % 8) == H_X + row_h // 8
    pad = jnp.zeros((8 - DEC_SEQ, D_MODEL), F32)

    def rotate_lanes(x, shift):
        return jnp.concatenate([pltpu.roll(x[:, g * LANES:(g + 1) * LANES], shift, 1)
                                for g in range(x.shape[1] // LANES)], axis=1)

    def stacked_queries(b):
        q8 = jnp.concatenate([q_ref[b], pad], axis=0)
        blocks = [q8[:, (hx * 2 + half) * LANES:(hx * 2 + half + 1) * LANES] for half in range(2) for hx in range(H_X)]
        return jnp.concatenate(blocks, axis=0).astype(BF16)

    scores = [lax.dot_general(stacked_queries(b), mk_ref[b].reshape(n_keys, LANES).astype(BF16), _NT,
                              preferred_element_type=F32) for b in range(XATT_DEC_BLOCK)]
    for b, s in enumerate(scores):
        part = jnp.where(own, s, 0.0)
        part = part[:H_X * 8] + part[H_X * 8:]
        full = jnp.where(summed, part + rotate_lanes(part, H_X), -jnp.inf)
        p = _softmax_rows(full)
        p2 = jnp.concatenate([rotate_lanes(p, LANES - H_X), p], axis=0).astype(BF16)
        o = _dot(p2, mv_ref[b].reshape(n_keys, LANES).astype(BF16))
        blocks = [o[(half * H_X + hx) * 8:(half * H_X + hx + 1) * 8] for hx in range(H_X) for half in range(2)]
        o_ref[b] = jnp.concatenate(blocks, axis=1)[:DEC_SEQ]


def _xatt_sample(q, mk, mv):
    tok = pl.BlockSpec((XATT_DEC_BLOCK, DEC_SEQ, D_MODEL), lambda i: (i, 0, 0))
    mem = pl.BlockSpec((XATT_DEC_BLOCK, N_MEM, 8, LANES), lambda i: (i, 0, 0, 0))
    return pl.pallas_call(
        _xatt_sample_body,
        grid=(DEC_BATCH // XATT_DEC_BLOCK,),
        in_specs=[tok, mem, mem],
        out_specs=tok,
        out_shape=jax.ShapeDtypeStruct((DEC_BATCH, DEC_SEQ, D_MODEL), F32),
        compiler_params=_params(1),
        name="xatt_sample",
    )(q, mk, mv)


def _xo_body(h_ref, o_ref, wxo_ref, g_ref, out_ref):
    out_ref[...] = h_ref[...] + _rms(_dot(o_ref[...].astype(BF16), wxo_ref[...]), g_ref[...])


def _xo(h, o, w_xo, g):
    m = h.shape[0]
    row = pl.BlockSpec((ROW_TILE, D_MODEL), lambda i: (i, 0))
    return pl.pallas_call(
        _xo_body,
        grid=(m // ROW_TILE,),
        in_specs=[row, row, _const_spec((D_MODEL, D_MODEL)), _const_spec((1, D_MODEL))],
        out_specs=row,
        out_shape=jax.ShapeDtypeStruct((m, D_MODEL), F32),
        compiler_params=_params(1),
        name="xo",
    )(h, o, w_xo, g)


def _memkv_body(m_ref, g_ref, wk_ref, wv_ref, k_ref, v_ref, kb_ref, vb_ref):
    mn = _rms(m_ref[...], g_ref[...]).astype(BF16)
    k = _dot(mn, wk_ref[...])
    v = _dot(mn, wv_ref[...])
    k_ref[...] = k
    v_ref[...] = v
    kb_ref[...] = k.astype(BF16)
    vb_ref[...] = v.astype(BF16)


def _memkv(mem, g, w_mk, w_mv):
    m = mem.shape[0]
    row = pl.BlockSpec((ROW_TILE, D_MODEL), lambda i: (i, 0))
    mat = _const_spec((D_MODEL, D_MODEL))
    return pl.pallas_call(
        _memkv_body,
        grid=(m // ROW_TILE,),
        in_specs=[row, _const_spec((1, D_MODEL)), mat, mat],
        out_specs=[row] * 4,
        out_shape=[jax.ShapeDtypeStruct((m, D_MODEL), F32)] * 2 + [jax.ShapeDtypeStruct((m, D_MODEL), BF16)] * 2,
        compiler_params=_params(1),
        name="memkv",
    )(mem, g, w_mk, w_mv)


def _rope_tables(pos):
    inv = ROPE_BASE ** (-jnp.arange(0, DK_RET, 2, dtype=F32) / DK_RET)
    ang = pos.astype(F32)[:, None] * inv[None, :]
    cos, sin = jnp.cos(ang), jnp.sin(ang)
    return jnp.concatenate([cos, cos], axis=1), jnp.concatenate([-sin, sin], axis=1)


def kernel(x_prompt, x_sample, mem_prompt, cache_ret_state, cache_sb_k, cache_sb_v, cache_mem_k, cache_mem_v,
           page_table, g_ffn1_pre, w_ffn1_gate, w_ffn1_up, w_ffn1_down, g_ffn1_post, g_mix_pre, w_in, b_sb,
           g_ret_gn, w_out, g_mix_post, g_mem, w_mk, w_mv, g_x_pre, w_xq, w_xo, g_x_post, g_ffn2_pre,
           w_ffn2_gate, w_ffn2_up, w_ffn2_down, g_ffn2_post):
    l = 0

    def wt(w):
        return w[l].astype(BF16)

    (g_ffn1_pre, g_ffn1_post, g_mix_pre, g_ret_gn, g_mix_post, g_mem, g_x_pre, g_x_post, g_ffn2_pre,
     g_ffn2_post) = (g[l][None, :] for g in (g_ffn1_pre, g_ffn1_post, g_mix_pre, g_ret_gn, g_mix_post, g_mem,
                                             g_x_pre, g_x_post, g_ffn2_pre, g_ffn2_post))
    ffn1 = (g_ffn1_pre, wt(w_ffn1_gate), wt(w_ffn1_up), wt(w_ffn1_down), g_ffn1_post)
    ffn2 = (g_ffn2_pre, wt(w_ffn2_gate), wt(w_ffn2_up), wt(w_ffn2_down), g_ffn2_post)
    w_in_b, w_out_b, w_xq_b, w_xo_b = wt(w_in), wt(w_out), wt(w_xq), wt(w_xo)
    lg = jnp.log1p(-jnp.exp2(-5.0 - jnp.arange(H_RET, dtype=F32)))
    bias = b_sb[l]
    cumsum_w = _cumsum_weights()

    xp = x_prompt.reshape(BATCH * SEQ, D_MODEL)
    h = _ffn(xp, *ffn1)
    cos_p, sin_p = _rope_tables(jnp.arange(SEQ))
    q_r, k_r, v_r, g_r, q_s, k_st, v_st, v_sb = _inproj(h, g_mix_pre, w_in_b, cos_p, sin_p,
                                                         (BF16, F32, BF16, F32, BF16), transposed=True)
    ret, rs_p = _ret_prompt(lg, q_r, k_r, v_r, g_r, g_ret_gn)
    o_s = _sb_prompt(bias, q_s, k_st, v_sb, cumsum_w)
    mk_p, mv_p, mk_b, mv_b = _memkv(mem_prompt.reshape(BATCH * N_MEM, D_MODEL), g_mem, wt(w_mk), wt(w_mv))
    h, q_x = _mix_xq(h, ret, o_s, w_out_b, g_mix_post, g_x_pre, w_xq_b, BF16)
    h = _xo(h, _xatt_prompt(q_x, mk_b, mv_b), w_xo_b, g_x_post)
    y_prompt = _ffn(h, *ffn2).reshape(BATCH, SEQ, D_MODEL)

    n_tok = DEC_BATCH * DEC_SEQ
    hs = _ffn(x_sample.reshape(n_tok, D_MODEL), *ffn1)
    cos_s, sin_s = _rope_tables(PAST_LEN + jnp.arange(DEC_SEQ))
    cos_s, sin_s = jnp.tile(cos_s, (DEC_BATCH, 1)), jnp.tile(sin_s, (DEC_BATCH, 1))
    sq_r, sk_r, sv_r, sg_r, sq_s, sk_s, sv_s = _inproj(hs, g_mix_pre, w_in_b, cos_s, sin_s, (F32,) * 5,
                                                       transposed=False)

    def per_seq(t):
        return t.reshape(DEC_BATCH, DEC_SEQ, t.shape[-1])

    s_ret, rs_s = _ret_sample(lg, per_seq(sq_r), per_seq(sk_r), per_seq(sv_r), per_seq(sg_r), g_ret_gn,
                              cache_ret_state[l])

    def pool_t(c):
        return c[l].transpose(0, 2, 3, 1)

    s_sb = _sb_sample(page_table.reshape(-1), bias, per_seq(sq_s), per_seq(sk_s), per_seq(sv_s), cumsum_w,
                      pool_t(cache_sb_k), pool_t(cache_sb_v))
    hs, sq_x = _mix_xq(hs, s_ret.reshape(n_tok, W_RET), s_sb.reshape(n_tok, W_SB), w_out_b, g_mix_post,
                       g_x_pre, w_xq_b, F32)
    def mem_rows(c):
        c = c[l].reshape(DEC_BATCH, N_MEM, H_X, 2, LANES)
        return c.transpose(0, 1, 3, 2, 4).reshape(DEC_BATCH, N_MEM, 2 * H_X, LANES)

    so = _xatt_sample(per_seq(sq_x), mem_rows(cache_mem_k), mem_rows(cache_mem_v))
    hs = _xo(hs, so.reshape(n_tok, D_MODEL), w_xo_b, g_x_post)
    y_sample = _ffn(hs, *ffn2).reshape(DEC_BATCH, DEC_SEQ, D_MODEL)

    def prompt_rows(t):
        return t.reshape(BATCH, H_SB, D_SB, SEQ).transpose(0, 3, 1, 2)[None]

    return (y_prompt, y_sample, rs_p[None],
            prompt_rows(k_st), prompt_rows(v_st),
            mk_p.reshape(1, BATCH, N_MEM, H_X, D_X), mv_p.reshape(1, BATCH, N_MEM, H_X, D_X),
            rs_s[None],
            sk_s.reshape(1, DEC_BATCH, DEC_SEQ, H_SB, D_SB), sv_s.reshape(1, DEC_BATCH, DEC_SEQ, H_SB, D_SB))
```

```python
import functools

import jax
import jax.numpy as jnp
import numpy as np
from jax import lax
from jax.experimental import pallas as pl
from jax.experimental.pallas import tpu as pltpu

F32 = jnp.float32
BF16 = jnp.bfloat16

D_MODEL = 1024
BATCH = 8
SEQ = 2048
DEC_BATCH = 128
DEC_SEQ = 4
PAST_LEN = 2048
PAGE_SIZE = 128
N_PAGES = PAST_LEN // PAGE_SIZE
W_RET = 512
DK_RET = 128
H_RET = 4
W_SB = 512
D_SB = 64
H_SB = 8
IN_COLS = 4 * W_RET + 3 * W_SB
N_MEM = 256
H_X = 4
D_X = 256
D_FF = 2816
CHUNK = 128
ROPE_BASE = 10000.0
EPS = 1e-6
LOG2E = 1.4426950408889634

LANES = 128
ROW_TILE = 512
FF_TILE = 256
SB_TILE = 256
SB_HEADS_PER_STEP = 4
RET_DEC_BLOCK = 8
XATT_DEC_BLOCK = 4
VMEM_LIMIT = 56 * 1024 * 1024

_NT = (((1,), (1,)), ((), ()))
_TN = (((0,), (0,)), ((), ()))


def _params(n_axes):
    return pltpu.CompilerParams(dimension_semantics=("arbitrary",) * n_axes, vmem_limit_bytes=VMEM_LIMIT)


def _const_spec(shape):
    zeros = (0,) * len(shape)
    return pl.BlockSpec(shape, lambda *_: zeros, pipeline_mode=pl.Buffered(1))


def _smem_spec():
    return pl.BlockSpec(memory_space=pltpu.SMEM)


def _rms(x, g):
    return x * lax.rsqrt(jnp.mean(x * x, axis=-1, keepdims=True) + EPS) * g


def _dot(a, b):
    return jnp.dot(a, b, preferred_element_type=F32)


def _silu(x):
    return x * jax.nn.sigmoid(x)


def _ffn_body(x_ref, gpre_ref, wg_ref, wu_ref, wd_ref, gpost_ref, o_ref, act_ref):
    x = x_ref[...]
    xn = _rms(x, gpre_ref[...]).astype(BF16)
    for c in range(D_FF // FF_TILE):
        sl = slice(c * FF_TILE, (c + 1) * FF_TILE)
        g = _dot(xn, wg_ref[:, sl])
        u = _dot(xn, wu_ref[:, sl])
        act_ref[:, sl] = (_silu(g) * u).astype(BF16)
    y = _dot(act_ref[...], wd_ref[...])
    o_ref[...] = x + 0.5 * _rms(y, gpost_ref[...])


def _ffn(x, g_pre, wg, wu, wd, g_post):
    m = x.shape[0]
    row = pl.BlockSpec((ROW_TILE, D_MODEL), lambda i: (i, 0))
    return pl.pallas_call(
        _ffn_body,
        grid=(m // ROW_TILE,),
        in_specs=[row, _const_spec((1, D_MODEL)), _const_spec((D_MODEL, D_FF)), _const_spec((D_MODEL, D_FF)),
                  _const_spec((D_FF, D_MODEL)), _const_spec((1, D_MODEL))],
        out_specs=row,
        out_shape=jax.ShapeDtypeStruct((m, D_MODEL), F32),
        scratch_shapes=[pltpu.VMEM((ROW_TILE, D_FF), BF16)],
        compiler_params=_params(1),
        name="ffn",
    )(x, g_pre, wg, wu, wd, g_post)


def _inproj_body(h_ref, g_ref, w_ref, cos_ref, sin_ref, qr_ref, kr_ref, vr_ref, gr_ref, qs_ref, ks_ref, vs_ref,
                 *vsb_ref, transposed):
    xn = _rms(h_ref[...], g_ref[...]).astype(BF16)
    cos2 = cos_ref[...]
    sin2 = sin_ref[...]

    def proj(k):
        return _dot(xn, w_ref[:, k * W_RET:(k + 1) * W_RET])

    def rope(p):
        outs = []
        for h in range(H_RET):
            x = p[:, h * DK_RET:(h + 1) * DK_RET]
            outs.append(x * cos2 + pltpu.roll(x, DK_RET // 2, 1) * sin2)
        return jnp.concatenate(outs, axis=1)

    qr_ref[...] = rope(proj(0)).astype(qr_ref.dtype)
    kr_ref[...] = (rope(proj(1)) * (DK_RET ** -0.5)).astype(kr_ref.dtype)
    vr_ref[...] = proj(2).astype(vr_ref.dtype)
    gr_ref[...] = proj(3).astype(gr_ref.dtype)
    qs_ref[...] = (proj(4) * (D_SB ** -0.5 * LOG2E)).astype(qs_ref.dtype)
    ks, vs = proj(5), proj(6)
    if transposed:
        ks_ref[0] = ks.T
        vs_ref[0] = vs.T
        vsb_ref[0][...] = vs.astype(BF16)
    else:
        ks_ref[...] = ks
        vs_ref[...] = vs


def _inproj(h, g, w, cos2, sin2, dtypes, transposed):
    m = h.shape[0]
    n_tab = cos2.shape[0] // ROW_TILE
    row = pl.BlockSpec((ROW_TILE, D_MODEL), lambda i: (i, 0))
    tab = pl.BlockSpec((ROW_TILE, LANES), lambda i: (i % n_tab, 0))
    out = pl.BlockSpec((ROW_TILE, W_RET), lambda i: (i, 0))
    outs = [out] * 5
    shapes = [jax.ShapeDtypeStruct((m, W_RET), dt) for dt in dtypes]
    if transposed:
        tiles = SEQ // ROW_TILE
        out_t = pl.BlockSpec((1, W_SB, ROW_TILE), lambda i: (i // tiles, 0, i % tiles))
        outs += [out_t, out_t, out]
        shapes += [jax.ShapeDtypeStruct((m // SEQ, W_SB, SEQ), F32)] * 2 + [jax.ShapeDtypeStruct((m, W_SB), BF16)]
    else:
        outs += [out, out]
        shapes += [jax.ShapeDtypeStruct((m, W_SB), F32)] * 2
    return pl.pallas_call(
        functools.partial(_inproj_body, transposed=transposed),
        grid=(m // ROW_TILE,),
        in_specs=[row, _const_spec((1, D_MODEL)), _const_spec((D_MODEL, IN_COLS)), tab, tab],
        out_specs=outs,
        out_shape=shapes,
        compiler_params=_params(1),
        name="inproj",
    )(h, g, w, cos2, sin2)


def _ret_decays(lg, length):
    i = lax.broadcasted_iota(jnp.int32, (CHUNK, CHUNK), 0).astype(F32)
    j = lax.broadcasted_iota(jnp.int32, (CHUNK, CHUNK), 1).astype(F32)
    diff = i - j
    causal = diff >= 0
    dmat = jnp.where(causal, jnp.exp(jnp.where(causal, diff, 0.0) * lg), 0.0)
    rowdec = jnp.exp((i + 1.0) * lg)
    kdec = jnp.exp((length - 1.0 - i) * lg)
    sdec = jnp.exp(jnp.full((CHUNK, CHUNK), float(length), F32) * lg)
    return dmat, rowdec, kdec, sdec


def _ret_chunk_heads(qkv, states, decays):
    first = []
    for (q, k, v), s, (_, _, kdec, _) in zip(qkv, states, decays):
        scores = lax.dot_general(q, k.astype(BF16), _NT, preferred_element_type=F32)
        cross = _dot(q, s.astype(BF16))
        update = lax.dot_general((k * kdec).astype(BF16), v, _TN, preferred_element_type=F32)
        first.append((scores, cross, update))
    outs = []
    for (q, k, v), s, (dmat, rowdec, _, sdec), (scores, cross, update) in zip(qkv, states, decays, first):
        inner = _dot((scores * dmat).astype(BF16), v)
        outs.append((inner + cross * rowdec, sdec * s + update))
    return outs


def _ret_gate(o, gr, ggn):
    mu = jnp.mean(o, axis=-1, keepdims=True)
    d = o - mu
    var = jnp.mean(d * d, axis=-1, keepdims=True)
    return _silu(gr) * (d * lax.rsqrt(var + EPS) * ggn)


def _ret_prompt_body(lg_ref, q_ref, k_ref, v_ref, gr_ref, ggn_ref, o_ref, s_ref):
    heads = [slice(h * DK_RET, (h + 1) * DK_RET) for h in range(H_RET)]
    decays = [_ret_decays(lg_ref[h], CHUNK) for h in range(H_RET)]

    def chunk(c, states):
        rows = pl.ds(pl.multiple_of(c * CHUNK, CHUNK), CHUNK)
        qkv = [(q_ref[rows, cols], k_ref[rows, cols], v_ref[rows, cols]) for cols in heads]
        new_states = []
        for cols, (o, s_new) in zip(heads, _ret_chunk_heads(qkv, states, decays)):
            o_ref[rows, cols] = _ret_gate(o, gr_ref[rows, cols], ggn_ref[:, cols]).astype(o_ref.dtype)
            new_states.append(s_new)
        return tuple(new_states)

    zero = jnp.zeros((CHUNK, CHUNK), F32)
    states = lax.fori_loop(0, SEQ // CHUNK, chunk, (zero,) * H_RET)
    for h in range(H_RET):
        s_ref[0, h] = states[h]


def _ret_prompt(lg, q, k, v, gr, ggn):
    blk = pl.BlockSpec((SEQ, W_RET), lambda b: (b, 0))
    return pl.pallas_call(
        _ret_prompt_body,
        grid=(BATCH,),
        in_specs=[_smem_spec(), blk, blk, blk, blk, _const_spec((1, W_RET))],
        out_specs=[blk, pl.BlockSpec((1, H_RET, DK_RET, DK_RET), lambda b: (b, 0, 0, 0))],
        out_shape=[jax.ShapeDtypeStruct((BATCH * SEQ, W_RET), BF16),
                   jax.ShapeDtypeStruct((BATCH, H_RET, DK_RET, DK_RET), F32)],
        compiler_params=_params(1),
        name="ret_prompt",
    )(lg, q, k, v, gr, ggn)


def _ret_sample_body(lg_ref, q_ref, k_ref, v_ref, gr_ref, ggn_ref, s0_ref, o_ref, s_ref):
    pad = jnp.zeros((CHUNK - DEC_SEQ, DK_RET), F32)
    heads = [slice(h * DK_RET, (h + 1) * DK_RET) for h in range(H_RET)]
    decays = [_ret_decays(lg_ref[h], DEC_SEQ) for h in range(H_RET)]
    for b in range(RET_DEC_BLOCK):
        def tile(ref, cols):
            return jnp.concatenate([ref[b, :, cols], pad], axis=0)

        qkv = [(tile(q_ref, cols).astype(BF16), tile(k_ref, cols), tile(v_ref, cols).astype(BF16)) for cols in heads]
        states = [s0_ref[b, h] for h in range(H_RET)]
        for h, (o, s_new) in enumerate(_ret_chunk_heads(qkv, states, decays)):
            s_ref[b, h] = s_new
            o_ref[b, :, heads[h]] = _ret_gate(o, tile(gr_ref, heads[h]), ggn_ref[:, heads[h]])[:DEC_SEQ]


def _ret_sample(lg, q, k, v, gr, ggn, s0):
    tok = pl.BlockSpec((RET_DEC_BLOCK, DEC_SEQ, W_RET), lambda i: (i, 0, 0))
    st = pl.BlockSpec((RET_DEC_BLOCK, H_RET, DK_RET, DK_RET), lambda i: (i, 0, 0, 0))
    return pl.pallas_call(
        _ret_sample_body,
        grid=(DEC_BATCH // RET_DEC_BLOCK,),
        in_specs=[_smem_spec(), tok, tok, tok, tok, _const_spec((1, W_RET)), st],
        out_specs=[tok, st],
        out_shape=[jax.ShapeDtypeStruct((DEC_BATCH, DEC_SEQ, W_RET), F32),
                   jax.ShapeDtypeStruct((DEC_BATCH, H_RET, DK_RET, DK_RET), F32)],
        compiler_params=_params(1),
        name="ret_sample",
    )(lg, q, k, v, gr, ggn, s0)


def _cumsum_weights():
    s_src = np.arange(LANES)[:, None]
    s_dst = np.arange(LANES)[None, :]
    half = np.concatenate([(s_src > s_dst).astype(np.float32), np.ones((LANES, LANES), np.float32)], axis=1)
    return jnp.asarray(np.concatenate([half, half], axis=0), dtype=BF16)


def _softplus2(z2):
    return jnp.maximum(z2, 0.0) + jnp.log2(1.0 + jnp.exp2(jnp.minimum(z2, -z2)))


def _split_bf16(x):
    hi = x.astype(BF16)
    lo = (x - hi.astype(F32)).astype(BF16)
    return hi, lo


def _sb_prompt_body(bias_ref, q_ref, kt_ref, v_ref, w_ref, o_ref):
    hp = pl.program_id(1)
    t = SB_TILE
    lane = lax.broadcasted_iota(jnp.int32, (t, LANES), 1)
    row = lax.broadcasted_iota(jnp.int32, (t, t), 0)
    col = lax.broadcasted_iota(jnp.int32, (t, t), 1)
    strictly_earlier = col < row
    w = w_ref[...]

    def log_weights(z, diagonal):
        sp = _softplus2(z)
        hi, lo = _split_bf16(jnp.where(strictly_earlier, sp, 0.0) if diagonal else sp)
        halves = []
        for half in (slice(0, LANES), slice(LANES, 2 * LANES)):
            r = _dot(jnp.concatenate([hi[:, half], lo[:, half]], axis=1), w)
            halves.append((z[:, half] - sp[:, half] - r[:, :LANES], r[:, LANES:]))
        return halves

    def attend(halves, vb, carry, diagonal):
        o, c = carry
        (logw1, total1), (logw2, total2) = halves
        a2 = jnp.exp2(logw2 - c)
        c = c + total2
        a1 = jnp.exp2(logw1 - c)
        c = c + total1
        a = jnp.concatenate([a1, a2], axis=1)
        if diagonal:
            a = jnp.where(strictly_earlier, a, 0.0)
        return o + _dot(a.astype(BF16), vb), c

    def q_tile(qi, _):
        rows = pl.ds(pl.multiple_of(qi * t, t), t)
        qms, biases = [], []
        for hh in range(SB_HEADS_PER_STEP):
            qt = q_ref[rows, (hh // 2) * LANES:(hh // 2 + 1) * LANES]
            own = (lane < D_SB) if hh % 2 == 0 else (lane >= D_SB)
            qms.append(jnp.where(own, qt, jnp.zeros_like(qt)))
            biases.append(bias_ref[SB_HEADS_PER_STEP * hp + hh] * LOG2E)

        def key_tiles(kjs, carries, diagonal=False):
            kvs = []
            for kj in kjs:
                krows = pl.ds(pl.multiple_of(kj * t, t), t)
                kvs.append([(kt_ref[0, g * LANES:(g + 1) * LANES, krows].astype(BF16),
                             v_ref[krows, g * LANES:(g + 1) * LANES]) for g in range(SB_HEADS_PER_STEP // 2)])
            zs = [[_dot(qms[hh], kv[hh // 2][0]) + biases[hh] for hh in range(SB_HEADS_PER_STEP)] for kv in kvs]
            parts = [[log_weights(z, diagonal) for z in zh] for zh in zs]
            carries = list(carries)
            for kv, ph in zip(kvs, parts):
                for hh in range(SB_HEADS_PER_STEP):
                    carries[hh] = attend(ph[hh], kv[hh // 2][1], carries[hh], diagonal)
            return tuple(carries)

        zero = jnp.zeros((t, LANES), F32)
        carries = key_tiles((qi,), ((zero, zero),) * SB_HEADS_PER_STEP, diagonal=True)
        carries = lax.cond(qi % 2 == 1, lambda cr: key_tiles((qi - 1,), cr), lambda cr: cr, carries)
        top = qi - 1 - qi % 2

        def key_tile_pair(n, carries):
            return key_tiles((top - 2 * n, top - 2 * n - 1), carries)

        carries = lax.fori_loop(0, qi // 2, key_tile_pair, carries)
        for g in range(SB_HEADS_PER_STEP // 2):
            o_ref[rows, g * LANES:(g + 1) * LANES] = jnp.where(
                lane < D_SB, carries[2 * g][0], carries[2 * g + 1][0]).astype(o_ref.dtype)
        return 0

    lax.fori_loop(0, SEQ // t, q_tile, 0)


def _sb_prompt(bias, q, k_t, v, w):
    width = SB_HEADS_PER_STEP * D_SB
    blk = pl.BlockSpec((SEQ, width), lambda b, hp: (b, hp))
    blk_t = pl.BlockSpec((1, width, SEQ), lambda b, hp: (b, hp, 0))
    return pl.pallas_call(
        _sb_prompt_body,
        grid=(BATCH, W_SB // width),
        in_specs=[_smem_spec(), blk, blk_t, blk, _const_spec((2 * LANES, 2 * LANES))],
        out_specs=blk,
        out_shape=jax.ShapeDtypeStruct((BATCH * SEQ, W_SB), BF16),
        compiler_params=_params(2),
        name="sb_prompt",
    )(bias, q, k_t, v, w)


def _sb_sample_body(pt_ref, bias_ref, q_ref, kn_ref, vn_ref, w_ref, *rest):
    kps = rest[:N_PAGES]
    vps = rest[N_PAGES:2 * N_PAGES]
    o_ref = rest[2 * N_PAGES]
    rows = H_SB * 8
    head_r = lax.broadcasted_iota(jnp.int32, (rows, W_SB), 0) // 8
    head_c = lax.broadcasted_iota(jnp.int32, (rows, W_SB), 1) // D_SB
    own = head_r == head_c
    q8 = jnp.concatenate([q_ref[0], jnp.zeros((8 - DEC_SEQ, W_SB), F32)], axis=0)
    qbd = jnp.where(own, jnp.concatenate([q8] * H_SB, axis=0), 0.0).astype(BF16)
    row128 = lax.broadcasted_iota(jnp.int32, (rows, LANES), 0)
    pos128 = lax.broadcasted_iota(jnp.int32, (rows, LANES), 1)
    bias = jnp.zeros((rows, LANES), F32)
    for h in range(H_SB):
        bias = jnp.where(row128 // 8 == h, bias_ref[h] * LOG2E, bias)
    new_mask = pos128 < (row128 % 8)
    w = w_ref[...]

    def log_weights(z, mask):
        z = z + bias
        sp = _softplus2(z)
        hi, lo = _split_bf16(sp if mask is None else jnp.where(mask, sp, 0.0))
        r = _dot(jnp.concatenate([hi, lo], axis=1), w)
        return z - sp - r[:, :LANES], r[:, LANES:]

    pad = jnp.zeros((PAGE_SIZE - DEC_SEQ, W_SB), F32)
    kn = jnp.concatenate([kn_ref[0], pad], axis=0).astype(BF16)
    vn = jnp.concatenate([vn_ref[0], pad], axis=0).astype(BF16)
    zs = [lax.dot_general(qbd, kn, _NT, preferred_element_type=F32)]
    zs += [_dot(qbd, kps[s][0].reshape(W_SB, PAGE_SIZE).astype(BF16)) for s in range(N_PAGES)]
    parts = [log_weights(z, new_mask if i == 0 else None) for i, z in enumerate(zs)]
    logw, c = parts[0]
    acc = _dot(jnp.where(new_mask, jnp.exp2(logw), 0.0).astype(BF16), vn)
    for s in range(N_PAGES):
        logw, total = parts[s + 1]
        a = jnp.exp2(logw - c)
        vt = vps[s][0].reshape(W_SB, PAGE_SIZE).astype(BF16)
        acc = acc + lax.dot_general(a.astype(BF16), vt, _NT, preferred_element_type=F32)
        c = c + total
    diag = jnp.where(own, acc, 0.0).reshape(H_SB, 8, W_SB)
    o_ref[0] = jnp.sum(diag, axis=0)[:DEC_SEQ]


def _sb_sample(pt, bias, q, k_new, v_new, w, kc_t, vc_t):
    tok = pl.BlockSpec((1, DEC_SEQ, W_SB), lambda b, pt: (b, 0, 0))

    def page_spec(s):
        return pl.BlockSpec((1, H_SB, D_SB, PAGE_SIZE), lambda b, pt: (pt[b * N_PAGES + (N_PAGES - 1 - s)], 0, 0, 0))

    pages = [page_spec(s) for s in range(N_PAGES)]
    grid_spec = pltpu.PrefetchScalarGridSpec(
        num_scalar_prefetch=1,
        grid=(DEC_BATCH,),
        in_specs=[_smem_spec(), tok, tok, tok, pl.BlockSpec((2 * LANES, 2 * LANES), lambda b, pt: (0, 0))]
        + pages + pages,
        out_specs=tok,
    )
    return pl.pallas_call(
        _sb_sample_body,
        grid_spec=grid_spec,
        out_shape=jax.ShapeDtypeStruct((DEC_BATCH, DEC_SEQ, W_SB), F32),
        compiler_params=_params(1),
        name="sb_sample",
    )(pt, bias, q, k_new, v_new, w, *([kc_t] * N_PAGES), *([vc_t] * N_PAGES))


def _mix_xq_body(h_ref, r_ref, s_ref, wout_ref, gpost_ref, gx_ref, wxq_ref, h2_ref, qx_ref):
    cat = jnp.concatenate([r_ref[...].astype(BF16), s_ref[...].astype(BF16)], axis=1)
    h2 = h_ref[...] + _rms(_dot(cat, wout_ref[...]), gpost_ref[...])
    h2_ref[...] = h2
    xn = _rms(h2, gx_ref[...]).astype(BF16)
    qx_ref[...] = (_dot(xn, wxq_ref[...]) * (D_X ** -0.5)).astype(qx_ref.dtype)


def _mix_xq(h, ret, sb, w_out, g_post, g_x, w_xq, q_dtype):
    m = h.shape[0]
    row = pl.BlockSpec((ROW_TILE, D_MODEL), lambda i: (i, 0))
    half = pl.BlockSpec((ROW_TILE, W_RET), lambda i: (i, 0))
    vec = _const_spec((1, D_MODEL))
    mat = _const_spec((D_MODEL, D_MODEL))
    return pl.pallas_call(
        _mix_xq_body,
        grid=(m // ROW_TILE,),
        in_specs=[row, half, half, mat, vec, vec, mat],
        out_specs=[row, row],
        out_shape=[jax.ShapeDtypeStruct((m, D_MODEL), F32), jax.ShapeDtypeStruct((m, D_MODEL), q_dtype)],
        compiler_params=_params(1),
        name="mix_xq",
    )(h, ret, sb, w_out, g_post, g_x, w_xq)


def _softmax_rows(s):
    e = jnp.exp(s - jnp.max(s, axis=-1, keepdims=True))
    return e / jnp.sum(e, axis=-1, keepdims=True)


def _mix_xatt_prompt_body(h_ref, r_ref, s_ref, wout_ref, gpost_ref, gx_ref, wxq_ref, mk_ref, mv_ref, wxo_ref,
                          gxpost_ref, out_ref):
    cat = jnp.concatenate([r_ref[...], s_ref[...]], axis=1)
    h2 = h_ref[...] + _rms(_dot(cat, wout_ref[...]), gpost_ref[...])
    xn = _rms(h2, gx_ref[...]).astype(BF16)
    q = (_dot(xn, wxq_ref[...]) * (D_X ** -0.5)).astype(BF16)
    heads = [slice(hx * D_X, (hx + 1) * D_X) for hx in range(H_X)]
    scores = [lax.dot_general(q[:, cols], mk_ref[:, cols], _NT, preferred_element_type=F32) for cols in heads]
    o = jnp.concatenate([_dot(_softmax_rows(s).astype(BF16), mv_ref[:, cols]).astype(BF16)
                         for s, cols in zip(scores, heads)], axis=1)
    out_ref[...] = h2 + _rms(_dot(o, wxo_ref[...]), gxpost_ref[...])


def _mix_xatt_prompt(h, ret, sb, w_out, g_post, g_x, w_xq, mk, mv, w_xo, g_xpost):
    m = h.shape[0]
    tiles_per_batch = SEQ // ROW_TILE
    row = pl.BlockSpec((ROW_TILE, D_MODEL), lambda i: (i, 0))
    half = pl.BlockSpec((ROW_TILE, W_RET), lambda i: (i, 0))
    mem = pl.BlockSpec((N_MEM, D_MODEL), lambda i: (i // tiles_per_batch, 0))
    vec = _const_spec((1, D_MODEL))
    mat = _const_spec((D_MODEL, D_MODEL))
    return pl.pallas_call(
        _mix_xatt_prompt_body,
        grid=(m // ROW_TILE,),
        in_specs=[row, half, half, mat, vec, vec, mat, mem, mem, mat, vec],
        out_specs=row,
        out_shape=jax.ShapeDtypeStruct((m, D_MODEL), F32),
        compiler_params=_params(1),
        name="mix_xatt_prompt",
    )(h, ret, sb, w_out, g_post, g_x, w_xq, mk, mv, w_xo, g_xpost)


def _xatt_sample_body(q_ref, mk_ref, mv_ref, o_ref):
    n_rows = 2 * H_X * 8
    n_keys = N_MEM * 8
    row = lax.broadcasted_iota(jnp.int32, (n_rows, n_keys), 0)
    col = lax.broadcasted_iota(jnp.int32, (n_rows, n_keys), 1)
    own = ((col // H_X) % 2 == row // (H_X * 8)) & (col % H_X == (row // 8) % H_X)
    row_h = lax.broadcasted_iota(jnp.int32, (H_X * 8, n_keys), 0)
    col_h = lax.broadcasted_iota(jnp.int32, (H_X * 8, n_keys), 1)
    summed = (col_h % 8) == H_X + row_h // 8
    pad = jnp.zeros((8 - DEC_SEQ, D_MODEL), F32)

    def rotate_lanes(x, shift):
        return jnp.concatenate([pltpu.roll(x[:, g * LANES:(g + 1) * LANES], shift, 1)
                                for g in range(x.shape[1] // LANES)], axis=1)

    def stacked_queries(b):
        q8 = jnp.concatenate([q_ref[b], pad], axis=0)
        blocks = [q8[:, (hx * 2 + half) * LANES:(hx * 2 + half + 1) * LANES] for half in range(2) for hx in range(H_X)]
        return jnp.concatenate(blocks, axis=0).astype(BF16)

    scores = [lax.dot_general(stacked_queries(b), mk_ref[b].reshape(n_keys, LANES).astype(BF16), _NT,
                              preferred_element_type=F32) for b in range(XATT_DEC_BLOCK)]
    for b, s in enumerate(scores):
        part = jnp.where(own, s, 0.0)
        part = part[:H_X * 8] + part[H_X * 8:]
        full = jnp.where(summed, part + rotate_lanes(part, H_X), -jnp.inf)
        p = _softmax_rows(full)
        p2 = jnp.concatenate([rotate_lanes(p, LANES - H_X), p], axis=0).astype(BF16)
        o = _dot(p2, mv_ref[b].reshape(n_keys, LANES).astype(BF16))
        blocks = [o[(half * H_X + hx) * 8:(half * H_X + hx + 1) * 8] for hx in range(H_X) for half in range(2)]
        o_ref[b] = jnp.concatenate(blocks, axis=1)[:DEC_SEQ]


def _xatt_sample(q, mk, mv):
    tok = pl.BlockSpec((XATT_DEC_BLOCK, DEC_SEQ, D_MODEL), lambda i: (i, 0, 0))
    mem = pl.BlockSpec((XATT_DEC_BLOCK, N_MEM, 8, LANES), lambda i: (i, 0, 0, 0))
    return pl.pallas_call(
        _xatt_sample_body,
        grid=(DEC_BATCH // XATT_DEC_BLOCK,),
        in_specs=[tok, mem, mem],
        out_specs=tok,
        out_shape=jax.ShapeDtypeStruct((DEC_BATCH, DEC_SEQ, D_MODEL), F32),
        compiler_params=_params(1),
        name="xatt_sample",
    )(q, mk, mv)


def _xo_body(h_ref, o_ref, wxo_ref, g_ref, out_ref):
    out_ref[...] = h_ref[...] + _rms(_dot(o_ref[...].astype(BF16), wxo_ref[...]), g_ref[...])


def _xo(h, o, w_xo, g):
    m = h.shape[0]
    row = pl.BlockSpec((ROW_TILE, D_MODEL), lambda i: (i, 0))
    return pl.pallas_call(
        _xo_body,
        grid=(m // ROW_TILE,),
        in_specs=[row, row, _const_spec((D_MODEL, D_MODEL)), _const_spec((1, D_MODEL))],
        out_specs=row,
        out_shape=jax.ShapeDtypeStruct((m, D_MODEL), F32),
        compiler_params=_params(1),
        name="xo",
    )(h, o, w_xo, g)


def _memkv_body(m_ref, g_ref, wk_ref, wv_ref, k_ref, v_ref, kb_ref, vb_ref):
    mn = _rms(m_ref[...], g_ref[...]).astype(BF16)
    k = _dot(mn, wk_ref[...])
    v = _dot(mn, wv_ref[...])
    k_ref[...] = k
    v_ref[...] = v
    kb_ref[...] = k.astype(BF16)
    vb_ref[...] = v.astype(BF16)


def _memkv(mem, g, w_mk, w_mv):
    m = mem.shape[0]
    row = pl.BlockSpec((ROW_TILE, D_MODEL), lambda i: (i, 0))
    mat = _const_spec((D_MODEL, D_MODEL))
    return pl.pallas_call(
        _memkv_body,
        grid=(m // ROW_TILE,),
        in_specs=[row, _const_spec((1, D_MODEL)), mat, mat],
        out_specs=[row] * 4,
        out_shape=[jax.ShapeDtypeStruct((m, D_MODEL), F32)] * 2 + [jax.ShapeDtypeStruct((m, D_MODEL), BF16)] * 2,
        compiler_params=_params(1),
        name="memkv",
    )(mem, g, w_mk, w_mv)


def _rope_tables(pos):
    inv = ROPE_BASE ** (-jnp.arange(0, DK_RET, 2, dtype=F32) / DK_RET)
    ang = pos.astype(F32)[:, None] * inv[None, :]
    cos, sin = jnp.cos(ang), jnp.sin(ang)
    return jnp.concatenate([cos, cos], axis=1), jnp.concatenate([-sin, sin], axis=1)


def kernel(x_prompt, x_sample, mem_prompt, cache_ret_state, cache_sb_k, cache_sb_v, cache_mem_k, cache_mem_v,
           page_table, g_ffn1_pre, w_ffn1_gate, w_ffn1_up, w_ffn1_down, g_ffn1_post, g_mix_pre, w_in, b_sb,
           g_ret_gn, w_out, g_mix_post, g_mem, w_mk, w_mv, g_x_pre, w_xq, w_xo, g_x_post, g_ffn2_pre,
           w_ffn2_gate, w_ffn2_up, w_ffn2_down, g_ffn2_post):
    l = 0

    def wt(w):
        return w[l].astype(BF16)

    (g_ffn1_pre, g_ffn1_post, g_mix_pre, g_ret_gn, g_mix_post, g_mem, g_x_pre, g_x_post, g_ffn2_pre,
     g_ffn2_post) = (g[l][None, :] for g in (g_ffn1_pre, g_ffn1_post, g_mix_pre, g_ret_gn, g_mix_post, g_mem,
                                             g_x_pre, g_x_post, g_ffn2_pre, g_ffn2_post))
    ffn1 = (g_ffn1_pre, wt(w_ffn1_gate), wt(w_ffn1_up), wt(w_ffn1_down), g_ffn1_post)
    ffn2 = (g_ffn2_pre, wt(w_ffn2_gate), wt(w_ffn2_up), wt(w_ffn2_down), g_ffn2_post)
    w_in_b, w_out_b, w_xq_b, w_xo_b = wt(w_in), wt(w_out), wt(w_xq), wt(w_xo)
    lg = jnp.log1p(-jnp.exp2(-5.0 - jnp.arange(H_RET, dtype=F32)))
    bias = b_sb[l]
    cumsum_w = _cumsum_weights()

    xp = x_prompt.reshape(BATCH * SEQ, D_MODEL)
    h = _ffn(xp, *ffn1)
    cos_p, sin_p = _rope_tables(jnp.arange(SEQ))
    q_r, k_r, v_r, g_r, q_s, k_st, v_st, v_sb = _inproj(h, g_mix_pre, w_in_b, cos_p, sin_p,
                                                         (BF16, F32, BF16, F32, BF16), transposed=True)
    ret, rs_p = _ret_prompt(lg, q_r, k_r, v_r, g_r, g_ret_gn)
    o_s = _sb_prompt(bias, q_s, k_st, v_sb, cumsum_w)
    mk_p, mv_p, mk_b, mv_b = _memkv(mem_prompt.reshape(BATCH * N_MEM, D_MODEL), g_mem, wt(w_mk), wt(w_mv))
    h = _mix_xatt_prompt(h, ret, o_s, w_out_b, g_mix_post, g_x_pre, w_xq_b, mk_b, mv_b, w_xo_b, g_x_post)
    y_prompt = _ffn(h, *ffn2).reshape(BATCH, SEQ, D_MODEL)

    n_tok = DEC_BATCH * DEC_SEQ
    hs = _ffn(x_sample.reshape(n_tok, D_MODEL), *ffn1)
    cos_s, sin_s = _rope_tables(PAST_LEN + jnp.arange(DEC_SEQ))
    cos_s, sin_s = jnp.tile(cos_s, (DEC_BATCH, 1)), jnp.tile(sin_s, (DEC_BATCH, 1))
    sq_r, sk_r, sv_r, sg_r, sq_s, sk_s, sv_s = _inproj(hs, g_mix_pre, w_in_b, cos_s, sin_s, (F32,) * 5,
                                                       transposed=False)

    def per_seq(t):
        return t.reshape(DEC_BATCH, DEC_SEQ, t.shape[-1])

    s_ret, rs_s = _ret_sample(lg, per_seq(sq_r), per_seq(sk_r), per_seq(sv_r), per_seq(sg_r), g_ret_gn,
                              cache_ret_state[l])

    def pool_t(c):
        return c[l].transpose(0, 2, 3, 1)

    s_sb = _sb_sample(page_table.reshape(-1), bias, per_seq(sq_s), per_seq(sk_s), per_seq(sv_s), cumsum_w,
                      pool_t(cache_sb_k), pool_t(cache_sb_v))
    hs, sq_x = _mix_xq(hs, s_ret.reshape(n_tok, W_RET), s_sb.reshape(n_tok, W_SB), w_out_b, g_mix_post,
                       g_x_pre, w_xq_b, F32)
    def mem_rows(c):
        c = c[l].reshape(DEC_BATCH, N_MEM, H_X, 2, LANES)
        return c.transpose(0, 1, 3, 2, 4).reshape(DEC_BATCH, N_MEM, 2 * H_X, LANES)

    so = _xatt_sample(per_seq(sq_x), mem_rows(cache_mem_k), mem_rows(cache_mem_v))
    hs = _xo(hs, so.reshape(n_tok, D_MODEL), w_xo_b, g_x_post)
    y_sample = _ffn(hs, *ffn2).reshape(DEC_BATCH, DEC_SEQ, D_MODEL)

    def prompt_rows(t):
        return t.reshape(BATCH, H_SB, D_SB, SEQ).transpose(0, 3, 1, 2)[None]

    return (y_prompt, y_sample, rs_p[None],
            prompt_rows(k_st), prompt_rows(v_st),
            mk_p.reshape(1, BATCH, N_MEM, H_X, D_X), mv_p.reshape(1, BATCH, N_MEM, H_X, D_X),
            rs_s[None],
            sk_s.reshape(1, DEC_BATCH, DEC_SEQ, H_SB, D_SB), sv_s.reshape(1, DEC_BATCH, DEC_SEQ, H_SB, D_SB))
```

```python
import functools

import jax
import jax.numpy as jnp
import numpy as np
from jax import lax
from jax.experimental import pallas as pl
from jax.experimental.pallas import tpu as pltpu

F32 = jnp.float32
BF16 = jnp.bfloat16

D_MODEL = 1024
BATCH = 8
SEQ = 2048
DEC_BATCH = 128
DEC_SEQ = 4
PAST_LEN = 2048
PAGE_SIZE = 128
N_PAGES = PAST_LEN // PAGE_SIZE
W_RET = 512
DK_RET = 128
H_RET = 4
W_SB = 512
D_SB = 64
H_SB = 8
IN_COLS = 4 * W_RET + 3 * W_SB
N_MEM = 256
H_X = 4
D_X = 256
D_FF = 2816
CHUNK = 128
ROPE_BASE = 10000.0
EPS = 1e-6
LOG2E = 1.4426950408889634

LANES = 128
ROW_TILE = 512
FF_TILE = 256
SB_TILE = 256
SB_HEADS_PER_STEP = 4
RET_DEC_BLOCK = 8
XATT_DEC_BLOCK = 4
VMEM_LIMIT = 56 * 1024 * 1024

_NT = (((1,), (1,)), ((), ()))
_TN = (((0,), (0,)), ((), ()))


def _params(n_axes):
    return pltpu.CompilerParams(dimension_semantics=("arbitrary",) * n_axes, vmem_limit_bytes=VMEM_LIMIT)


def _const_spec(shape):
    zeros = (0,) * len(shape)
    return pl.BlockSpec(shape, lambda *_: zeros, pipeline_mode=pl.Buffered(1))


def _smem_spec():
    return pl.BlockSpec(memory_space=pltpu.SMEM)


def _rms(x, g):
    return x * lax.rsqrt(jnp.mean(x * x, axis=-1, keepdims=True) + EPS) * g


def _dot(a, b):
    return jnp.dot(a, b, preferred_element_type=F32)


def _silu(x):
    return x * jax.nn.sigmoid(x)


def _ffn_body(x_ref, gpre_ref, wg_ref, wu_ref, wd_ref, gpost_ref, o_ref, act_ref):
    x = x_ref[...]
    xn = _rms(x, gpre_ref[...]).astype(BF16)
    for c in range(D_FF // FF_TILE):
        sl = slice(c * FF_TILE, (c + 1) * FF_TILE)
        g = _dot(xn, wg_ref[:, sl])
        u = _dot(xn, wu_ref[:, sl])
        act_ref[:, sl] = (_silu(g) * u).astype(BF16)
    y = _dot(act_ref[...], wd_ref[...])
    o_ref[...] = x + 0.5 * _rms(y, gpost_ref[...])


def _ffn(x, g_pre, wg, wu, wd, g_post):
    m = x.shape[0]
    row = pl.BlockSpec((ROW_TILE, D_MODEL), lambda i: (i, 0))
    return pl.pallas_call(
        _ffn_body,
        grid=(m // ROW_TILE,),
        in_specs=[row, _const_spec((1, D_MODEL)), _const_spec((D_MODEL, D_FF)), _const_spec((D_MODEL, D_FF)),
                  _const_spec((D_FF, D_MODEL)), _const_spec((1, D_MODEL))],
        out_specs=row,
        out_shape=jax.ShapeDtypeStruct((m, D_MODEL), F32),
        scratch_shapes=[pltpu.VMEM((ROW_TILE, D_FF), BF16)],
        compiler_params=_params(1),
        name="ffn",
    )(x, g_pre, wg, wu, wd, g_post)


def _inproj_body(h_ref, g_ref, w_ref, cos_ref, sin_ref, qr_ref, kr_ref, vr_ref, gr_ref, qs_ref, ks_ref, vs_ref,
                 *vsb_ref, transposed):
    xn = _rms(h_ref[...], g_ref[...]).astype(BF16)
    cos2 = cos_ref[...]
    sin2 = sin_ref[...]

    def proj(k):
        return _dot(xn, w_ref[:, k * W_RET:(k + 1) * W_RET])

    def rope(p):
        outs = []
        for h in range(H_RET):
            x = p[:, h * DK_RET:(h + 1) * DK_RET]
            outs.append(x * cos2 + pltpu.roll(x, DK_RET // 2, 1) * sin2)
        return jnp.concatenate(outs, axis=1)

    qr_ref[...] = rope(proj(0)).astype(qr_ref.dtype)
    kr_ref[...] = (rope(proj(1)) * (DK_RET ** -0.5)).astype(kr_ref.dtype)
    vr_ref[...] = proj(2).astype(vr_ref.dtype)
    gr_ref[...] = proj(3).astype(gr_ref.dtype)
    qs_ref[...] = (proj(4) * (D_SB ** -0.5 * LOG2E)).astype(qs_ref.dtype)
    ks, vs = proj(5), proj(6)
    if transposed:
        ks_ref[0] = ks.T
        vs_ref[0] = vs.T
        vsb_ref[0][...] = vs.astype(BF16)
    else:
        ks_ref[...] = ks
        vs_ref[...] = vs


def _inproj(h, g, w, cos2, sin2, dtypes, transposed):
    m = h.shape[0]
    n_tab = cos2.shape[0] // ROW_TILE
    row = pl.BlockSpec((ROW_TILE, D_MODEL), lambda i: (i, 0))
    tab = pl.BlockSpec((ROW_TILE, LANES), lambda i: (i % n_tab, 0))
    out = pl.BlockSpec((ROW_TILE, W_RET), lambda i: (i, 0))
    outs = [out] * 5
    shapes = [jax.ShapeDtypeStruct((m, W_RET), dt) for dt in dtypes]
    if transposed:
        tiles = SEQ // ROW_TILE
        out_t = pl.BlockSpec((1, W_SB, ROW_TILE), lambda i: (i // tiles, 0, i % tiles))
        outs += [out_t, out_t, out]
        shapes += [jax.ShapeDtypeStruct((m // SEQ, W_SB, SEQ), F32)] * 2 + [jax.ShapeDtypeStruct((m, W_SB), BF16)]
    else:
        outs += [out, out]
        shapes += [jax.ShapeDtypeStruct((m, W_SB), F32)] * 2
    return pl.pallas_call(
        functools.partial(_inproj_body, transposed=transposed),
        grid=(m // ROW_TILE,),
        in_specs=[row, _const_spec((1, D_MODEL)), _const_spec((D_MODEL, IN_COLS)), tab, tab],
        out_specs=outs,
        out_shape=shapes,
        compiler_params=_params(1),
        name="inproj",
    )(h, g, w, cos2, sin2)


def _ret_decays(lg, length):
    i = lax.broadcasted_iota(jnp.int32, (CHUNK, CHUNK), 0).astype(F32)
    j = lax.broadcasted_iota(jnp.int32, (CHUNK, CHUNK), 1).astype(F32)
    diff = i - j
    causal = diff >= 0
    dmat = jnp.where(causal, jnp.exp(jnp.where(causal, diff, 0.0) * lg), 0.0)
    rowdec = jnp.exp((i + 1.0) * lg)
    kdec = jnp.exp((length - 1.0 - i) * lg)
    sdec = jnp.exp(jnp.full((CHUNK, CHUNK), float(length), F32) * lg)
    return dmat, rowdec, kdec, sdec


def _ret_chunk_heads(qkv, states, decays):
    first = []
    for (q, k, v), s, (_, _, kdec, _) in zip(qkv, states, decays):
        scores = lax.dot_general(q, k.astype(BF16), _NT, preferred_element_type=F32)
        cross = _dot(q, s.astype(BF16))
        update = lax.dot_general((k * kdec).astype(BF16), v, _TN, preferred_element_type=F32)
        first.append((scores, cross, update))
    outs = []
    for (q, k, v), s, (dmat, rowdec, _, sdec), (scores, cross, update) in zip(qkv, states, decays, first):
        inner = _dot((scores * dmat).astype(BF16), v)
        outs.append((inner + cross * rowdec, sdec * s + update))
    return outs


def _ret_gate(o, gr, ggn):
    mu = jnp.mean(o, axis=-1, keepdims=True)
    d = o - mu
    var = jnp.mean(d * d, axis=-1, keepdims=True)
    return _silu(gr) * (d * lax.rsqrt(var + EPS) * ggn)


def _ret_prompt_body(lg_ref, q_ref, k_ref, v_ref, gr_ref, ggn_ref, o_ref, s_ref):
    heads = [slice(h * DK_RET, (h + 1) * DK_RET) for h in range(H_RET)]
    decays = [_ret_decays(lg_ref[h], CHUNK) for h in range(H_RET)]

    def chunk(c, states):
        rows = pl.ds(pl.multiple_of(c * CHUNK, CHUNK), CHUNK)
        qkv = [(q_ref[rows, cols], k_ref[rows, cols], v_ref[rows, cols]) for cols in heads]
        new_states = []
        for cols, (o, s_new) in zip(heads, _ret_chunk_heads(qkv, states, decays)):
            o_ref[rows, cols] = _ret_gate(o, gr_ref[rows, cols], ggn_ref[:, cols]).astype(o_ref.dtype)
            new_states.append(s_new)
        return tuple(new_states)

    zero = jnp.zeros((CHUNK, CHUNK), F32)
    states = lax.fori_loop(0, SEQ // CHUNK, chunk, (zero,) * H_RET)
    for h in range(H_RET):
        s_ref[0, h] = states[h]


def _ret_prompt(lg, q, k, v, gr, ggn):
    blk = pl.BlockSpec((SEQ, W_RET), lambda b: (b, 0))
    return pl.pallas_call(
        _ret_prompt_body,
        grid=(BATCH,),
        in_specs=[_smem_spec(), blk, blk, blk, blk, _const_spec((1, W_RET))],
        out_specs=[blk, pl.BlockSpec((1, H_RET, DK_RET, DK_RET), lambda b: (b, 0, 0, 0))],
        out_shape=[jax.ShapeDtypeStruct((BATCH * SEQ, W_RET), BF16),
                   jax.ShapeDtypeStruct((BATCH, H_RET, DK_RET, DK_RET), F32)],
        compiler_params=_params(1),
        name="ret_prompt",
    )(lg, q, k, v, gr, ggn)


def _ret_sample_body(lg_ref, q_ref, k_ref, v_ref, gr_ref, ggn_ref, s0_ref, o_ref, s_ref):
    pad = jnp.zeros((CHUNK - DEC_SEQ, DK_RET), F32)
    heads = [slice(h * DK_RET, (h + 1) * DK_RET) for h in range(H_RET)]
    decays = [_ret_decays(lg_ref[h], DEC_SEQ) for h in range(H_RET)]
    for b in range(RET_DEC_BLOCK):
        def tile(ref, cols):
            return jnp.concatenate([ref[b, :, cols], pad], axis=0)

        qkv = [(tile(q_ref, cols).astype(BF16), tile(k_ref, cols), tile(v_ref, cols).astype(BF16)) for cols in heads]
        states = [s0_ref[b, h] for h in range(H_RET)]
        for h, (o, s_new) in enumerate(_ret_chunk_heads(qkv, states, decays)):
            s_ref[b, h] = s_new
            o_ref[b, :, heads[h]] = _ret_gate(o, tile(gr_ref, heads[h]), ggn_ref[:, heads[h]])[:DEC_SEQ]


def _ret_sample(lg, q, k, v, gr, ggn, s0):
    tok = pl.BlockSpec((RET_DEC_BLOCK, DEC_SEQ, W_RET), lambda i: (i, 0, 0))
    st = pl.BlockSpec((RET_DEC_BLOCK, H_RET, DK_RET, DK_RET), lambda i: (i, 0, 0, 0))
    return pl.pallas_call(
        _ret_sample_body,
        grid=(DEC_BATCH // RET_DEC_BLOCK,),
        in_specs=[_smem_spec(), tok, tok, tok, tok, _const_spec((1, W_RET)), st],
        out_specs=[tok, st],
        out_shape=[jax.ShapeDtypeStruct((DEC_BATCH, DEC_SEQ, W_RET), F32),
                   jax.ShapeDtypeStruct((DEC_BATCH, H_RET, DK_RET, DK_RET), F32)],
        compiler_params=_params(1),
        name="ret_sample",
    )(lg, q, k, v, gr, ggn, s0)


def _cumsum_weights():
    s_src = np.arange(LANES)[:, None]
    s_dst = np.arange(LANES)[None, :]
    half = np.concatenate([(s_src > s_dst).astype(np.float32), np.ones((LANES, LANES), np.float32)], axis=1)
    return jnp.asarray(np.concatenate([half, half], axis=0), dtype=BF16)


def _softplus2(z2):
    return jnp.maximum(z2, 0.0) + jnp.log2(1.0 + jnp.exp2(jnp.minimum(z2, -z2)))


def _split_bf16(x):
    hi = x.astype(BF16)
    lo = (x - hi.astype(F32)).astype(BF16)
    return hi, lo


def _sb_prompt_body(bias_ref, q_ref, kt_ref, v_ref, w_ref, o_ref):
    hp = pl.program_id(1)
    t = SB_TILE
    lane = lax.broadcasted_iota(jnp.int32, (t, LANES), 1)
    row = lax.broadcasted_iota(jnp.int32, (t, t), 0)
    col = lax.broadcasted_iota(jnp.int32, (t, t), 1)
    strictly_earlier = col < row
    w = w_ref[...]

    def log_weights(z, diagonal):
        sp = _softplus2(z)
        hi, lo = _split_bf16(jnp.where(strictly_earlier, sp, 0.0) if diagonal else sp)
        halves = []
        for half in (slice(0, LANES), slice(LANES, 2 * LANES)):
            r = _dot(jnp.concatenate([hi[:, half], lo[:, half]], axis=1), w)
            halves.append((z[:, half] - sp[:, half] - r[:, :LANES], r[:, LANES:]))
        return halves

    def attend(halves, vb, carry, diagonal):
        o, c = carry
        (logw1, total1), (logw2, total2) = halves
        a2 = jnp.exp2(logw2 - c)
        c = c + total2
        a1 = jnp.exp2(logw1 - c)
        c = c + total1
        a = jnp.concatenate([a1, a2], axis=1)
        if diagonal:
            a = jnp.where(strictly_earlier, a, 0.0)
        return o + _dot(a.astype(BF16), vb), c

    def q_tile(qi, _):
        rows = pl.ds(pl.multiple_of(qi * t, t), t)
        qms, biases = [], []
        for hh in range(SB_HEADS_PER_STEP):
            qt = q_ref[rows, (hh // 2) * LANES:(hh // 2 + 1) * LANES]
            own = (lane < D_SB) if hh % 2 == 0 else (lane >= D_SB)
            qms.append(jnp.where(own, qt, jnp.zeros_like(qt)))
            biases.append(bias_ref[SB_HEADS_PER_STEP * hp + hh] * LOG2E)

        def key_tiles(kjs, carries, diagonal=False):
            kvs = []
            for kj in kjs:
                krows = pl.ds(pl.multiple_of(kj * t, t), t)
                kvs.append([(kt_ref[0, g * LANES:(g + 1) * LANES, krows].astype(BF16),
                             v_ref[krows, g * LANES:(g + 1) * LANES]) for g in range(SB_HEADS_PER_STEP // 2)])
            zs = [[_dot(qms[hh], kv[hh // 2][0]) + biases[hh] for hh in range(SB_HEADS_PER_STEP)] for kv in kvs]
            parts = [[log_weights(z, diagonal) for z in zh] for zh in zs]
            carries = list(carries)
            for kv, ph in zip(kvs, parts):
                for hh in range(SB_HEADS_PER_STEP):
                    carries[hh] = attend(ph[hh], kv[hh // 2][1], carries[hh], diagonal)
            return tuple(carries)

        zero = jnp.zeros((t, LANES), F32)
        carries = key_tiles((qi,), ((zero, zero),) * SB_HEADS_PER_STEP, diagonal=True)
        carries = lax.cond(qi % 2 == 1, lambda cr: key_tiles((qi - 1,), cr), lambda cr: cr, carries)
        top = qi - 1 - qi % 2

        def key_tile_pair(n, carries):
            return key_tiles((top - 2 * n, top - 2 * n - 1), carries)

        carries = lax.fori_loop(0, qi // 2, key_tile_pair, carries)
        for g in range(SB_HEADS_PER_STEP // 2):
            o_ref[rows, g * LANES:(g + 1) * LANES] = jnp.where(
                lane < D_SB, carries[2 * g][0], carries[2 * g + 1][0]).astype(o_ref.dtype)
        return 0

    lax.fori_loop(0, SEQ // t, q_tile, 0)


def _sb_prompt(bias, q, k_t, v, w):
    width = SB_HEADS_PER_STEP * D_SB
    blk = pl.BlockSpec((SEQ, width), lambda b, hp: (b, hp))
    blk_t = pl.BlockSpec((1, width, SEQ), lambda b, hp: (b, hp, 0))
    return pl.pallas_call(
        _sb_prompt_body,
        grid=(BATCH, W_SB // width),
        in_specs=[_smem_spec(), blk, blk_t, blk, _const_spec((2 * LANES, 2 * LANES))],
        out_specs=blk,
        out_shape=jax.ShapeDtypeStruct((BATCH * SEQ, W_SB), BF16),
        compiler_params=_params(2),
        name="sb_prompt",
    )(bias, q, k_t, v, w)


def _sb_sample_body(pt_ref, bias_ref, q_ref, kn_ref, vn_ref, w_ref, *rest):
    kps = rest[:N_PAGES]
    vps = rest[N_PAGES:2 * N_PAGES]
    o_ref = rest[2 * N_PAGES]
    rows = H_SB * 8
    head_r = lax.broadcasted_iota(jnp.int32, (rows, W_SB), 0) // 8
    head_c = lax.broadcasted_iota(jnp.int32, (rows, W_SB), 1) // D_SB
    own = head_r == head_c
    q8 = jnp.concatenate([q_ref[0], jnp.zeros((8 - DEC_SEQ, W_SB), F32)], axis=0)
    qbd = jnp.where(own, jnp.concatenate([q8] * H_SB, axis=0), 0.0).astype(BF16)
    row128 = lax.broadcasted_iota(jnp.int32, (rows, LANES), 0)
    pos128 = lax.broadcasted_iota(jnp.int32, (rows, LANES), 1)
    bias = jnp.zeros((rows, LANES), F32)
    for h in range(H_SB):
        bias = jnp.where(row128 // 8 == h, bias_ref[h] * LOG2E, bias)
    new_mask = pos128 < (row128 % 8)
    w = w_ref[...]

    def log_weights(z, mask):
        z = z + bias
        sp = _softplus2(z)
        hi, lo = _split_bf16(sp if mask is None else jnp.where(mask, sp, 0.0))
        r = _dot(jnp.concatenate([hi, lo], axis=1), w)
        return z - sp - r[:, :LANES], r[:, LANES:]

    pad = jnp.zeros((PAGE_SIZE - DEC_SEQ, W_SB), F32)
    kn = jnp.concatenate([kn_ref[0], pad], axis=0).astype(BF16)
    vn = jnp.concatenate([vn_ref[0], pad], axis=0).astype(BF16)
    zs = [lax.dot_general(qbd, kn, _NT, preferred_element_type=F32)]
    zs += [_dot(qbd, kps[s][0].reshape(W_SB, PAGE_SIZE).astype(BF16)) for s in range(N_PAGES)]
    parts = [log_weights(z, new_mask if i == 0 else None) for i, z in enumerate(zs)]
    logw, c = parts[0]
    acc = _dot(jnp.where(new_mask, jnp.exp2(logw), 0.0).astype(BF16), vn)
    for s in range(N_PAGES):
        logw, total = parts[s + 1]
        a = jnp.exp2(logw - c)
        vt = vps[s][0].reshape(W_SB, PAGE_SIZE).astype(BF16)
        acc = acc + lax.dot_general(a.astype(BF16), vt, _NT, preferred_element_type=F32)
        c = c + total
    diag = jnp.where(own, acc, 0.0).reshape(H_SB, 8, W_SB)
    o_ref[0] = jnp.sum(diag, axis=0)[:DEC_SEQ]


def _sb_sample(pt, bias, q, k_new, v_new, w, kc_t, vc_t):
    tok = pl.BlockSpec((1, DEC_SEQ, W_SB), lambda b, pt: (b, 0, 0))

    def page_spec(s):
        return pl.BlockSpec((1, H_SB, D_SB, PAGE_SIZE), lambda b, pt: (pt[b * N_PAGES + (N_PAGES - 1 - s)], 0, 0, 0))

    pages = [page_spec(s) for s in range(N_PAGES)]
    grid_spec = pltpu.PrefetchScalarGridSpec(
        num_scalar_prefetch=1,
        grid=(DEC_BATCH,),
        in_specs=[_smem_spec(), tok, tok, tok, pl.BlockSpec((2 * LANES, 2 * LANES), lambda b, pt: (0, 0))]
        + pages + pages,
        out_specs=tok,
    )
    return pl.pallas_call(
        _sb_sample_body,
        grid_spec=grid_spec,
        out_shape=jax.ShapeDtypeStruct((DEC_BATCH, DEC_SEQ, W_SB), F32),
        compiler_params=_params(1),
        name="sb_sample",
    )(pt, bias, q, k_new, v_new, w, *([kc_t] * N_PAGES), *([vc_t] * N_PAGES))


def _mix_xq_body(h_ref, r_ref, s_ref, wout_ref, gpost_ref, gx_ref, wxq_ref, h2_ref, qx_ref):
    cat = jnp.concatenate([r_ref[...].astype(BF16), s_ref[...].astype(BF16)], axis=1)
    h2 = h_ref[...] + _rms(_dot(cat, wout_ref[...]), gpost_ref[...])
    h2_ref[...] = h2
    xn = _rms(h2, gx_ref[...]).astype(BF16)
    qx_ref[...] = (_dot(xn, wxq_ref[...]) * (D_X ** -0.5)).astype(qx_ref.dtype)


def _mix_xq(h, ret, sb, w_out, g_post, g_x, w_xq, q_dtype):
    m = h.shape[0]
    row = pl.BlockSpec((ROW_TILE, D_MODEL), lambda i: (i, 0))
    half = pl.BlockSpec((ROW_TILE, W_RET), lambda i: (i, 0))
    vec = _const_spec((1, D_MODEL))
    mat = _const_spec((D_MODEL, D_MODEL))
    return pl.pallas_call(
        _mix_xq_body,
        grid=(m // ROW_TILE,),
        in_specs=[row, half, half, mat, vec, vec, mat],
        out_specs=[row, row],
        out_shape=[jax.ShapeDtypeStruct((m, D_MODEL), F32), jax.ShapeDtypeStruct((m, D_MODEL), q_dtype)],
        compiler_params=_params(1),
        name="mix_xq",
    )(h, ret, sb, w_out, g_post, g_x, w_xq)


def _softmax_rows(s):
    e = jnp.exp(s - jnp.max(s, axis=-1, keepdims=True))
    return e / jnp.sum(e, axis=-1, keepdims=True)


def _mix_xatt_prompt_body(h_ref, r_ref, s_ref, wout_ref, gpost_ref, gx_ref, wxq_ref, mk_ref, mv_ref, wxo_ref,
                          gxpost_ref, out_ref):
    heads = [slice(hx * D_X, (hx + 1) * D_X) for hx in range(H_X)]
    halves = [slice(i * (ROW_TILE // 2), (i + 1) * (ROW_TILE // 2)) for i in range(2)]
    mixed = [_dot(jnp.concatenate([r_ref[rows, :], s_ref[rows, :]], axis=1), wout_ref[...]) for rows in halves]
    h2 = [h_ref[rows, :] + _rms(y, gpost_ref[...]) for rows, y in zip(halves, mixed)]
    q = [(_dot(_rms(x, gx_ref[...]).astype(BF16), wxq_ref[...]) * (D_X ** -0.5)).astype(BF16)
         for x in h2]
    scores = [[lax.dot_general(qh[:, cols], mk_ref[:, cols], _NT, preferred_element_type=F32) for cols in heads]
              for qh in q]
    o = [jnp.concatenate([_dot(_softmax_rows(s).astype(BF16), mv_ref[:, cols]).astype(BF16)
                          for s, cols in zip(sh, heads)], axis=1) for sh in scores]
    y = [_dot(oh, wxo_ref[...]) for oh in o]
    for rows, x, yh in zip(halves, h2, y):
        out_ref[rows, :] = x + _rms(yh, gxpost_ref[...])


def _mix_xatt_prompt(h, ret, sb, w_out, g_post, g_x, w_xq, mk, mv, w_xo, g_xpost):
    m = h.shape[0]
    tiles_per_batch = SEQ // ROW_TILE
    row = pl.BlockSpec((ROW_TILE, D_MODEL), lambda i: (i, 0))
    half = pl.BlockSpec((ROW_TILE, W_RET), lambda i: (i, 0))
    mem = pl.BlockSpec((N_MEM, D_MODEL), lambda i: (i // tiles_per_batch, 0))
    vec = _const_spec((1, D_MODEL))
    mat = _const_spec((D_MODEL, D_MODEL))
    return pl.pallas_call(
        _mix_xatt_prompt_body,
        grid=(m // ROW_TILE,),
        in_specs=[row, half, half, mat, vec, vec, mat, mem, mem, mat, vec],
        out_specs=row,
        out_shape=jax.ShapeDtypeStruct((m, D_MODEL), F32),
        compiler_params=_params(1),
        name="mix_xatt_prompt",
    )(h, ret, sb, w_out, g_post, g_x, w_xq, mk, mv, w_xo, g_xpost)


def _xatt_sample_body(q_ref, mk_ref, mv_ref, o_ref):
    n_rows = 2 * H_X * 8
    n_keys = N_MEM * 8
    row = lax.broadcasted_iota(jnp.int32, (n_rows, n_keys), 0)
    col = lax.broadcasted_iota(jnp.int32, (n_rows, n_keys), 1)
    own = ((col // H_X) % 2 == row // (H_X * 8)) & (col % H_X == (row // 8) % H_X)
    row_h = lax.broadcasted_iota(jnp.int32, (H_X * 8, n_keys), 0)
    col_h = lax.broadcasted_iota(jnp.int32, (H_X * 8, n_keys), 1)
    summed = (col_h % 8) == H_X + row_h // 8
    pad = jnp.zeros((8 - DEC_SEQ, D_MODEL), F32)

    def rotate_lanes(x, shift):
        return jnp.concatenate([pltpu.roll(x[:, g * LANES:(g + 1) * LANES], shift, 1)
                                for g in range(x.shape[1] // LANES)], axis=1)

    def stacked_queries(b):
        q8 = jnp.concatenate([q_ref[b], pad], axis=0)
        blocks = [q8[:, (hx * 2 + half) * LANES:(hx * 2 + half + 1) * LANES] for half in range(2) for hx in range(H_X)]
        return jnp.concatenate(blocks, axis=0).astype(BF16)

    scores = [lax.dot_general(stacked_queries(b), mk_ref[b].reshape(n_keys, LANES).astype(BF16), _NT,
                              preferred_element_type=F32) for b in range(XATT_DEC_BLOCK)]
    for b, s in enumerate(scores):
        part = jnp.where(own, s, 0.0)
        part = part[:H_X * 8] + part[H_X * 8:]
        full = jnp.where(summed, part + rotate_lanes(part, H_X), -jnp.inf)
        p = _softmax_rows(full)
        p2 = jnp.concatenate([rotate_lanes(p, LANES - H_X), p], axis=0).astype(BF16)
        o = _dot(p2, mv_ref[b].reshape(n_keys, LANES).astype(BF16))
        blocks = [o[(half * H_X + hx) * 8:(half * H_X + hx + 1) * 8] for hx in range(H_X) for half in range(2)]
        o_ref[b] = jnp.concatenate(blocks, axis=1)[:DEC_SEQ]


def _xatt_sample(q, mk, mv):
    tok = pl.BlockSpec((XATT_DEC_BLOCK, DEC_SEQ, D_MODEL), lambda i: (i, 0, 0))
    mem = pl.BlockSpec((XATT_DEC_BLOCK, N_MEM, 8, LANES), lambda i: (i, 0, 0, 0))
    return pl.pallas_call(
        _xatt_sample_body,
        grid=(DEC_BATCH // XATT_DEC_BLOCK,),
        in_specs=[tok, mem, mem],
        out_specs=tok,
        out_shape=jax.ShapeDtypeStruct((DEC_BATCH, DEC_SEQ, D_MODEL), F32),
        compiler_params=_params(1),
        name="xatt_sample",
    )(q, mk, mv)


def _xo_body(h_ref, o_ref, wxo_ref, g_ref, out_ref):
    out_ref[...] = h_ref[...] + _rms(_dot(o_ref[...].astype(BF16), wxo_ref[...]), g_ref[...])


def _xo(h, o, w_xo, g):
    m = h.shape[0]
    row = pl.BlockSpec((ROW_TILE, D_MODEL), lambda i: (i, 0))
    return pl.pallas_call(
        _xo_body,
        grid=(m // ROW_TILE,),
        in_specs=[row, row, _const_spec((D_MODEL, D_MODEL)), _const_spec((1, D_MODEL))],
        out_specs=row,
        out_shape=jax.ShapeDtypeStruct((m, D_MODEL), F32),
        compiler_params=_params(1),
        name="xo",
    )(h, o, w_xo, g)


def _memkv_body(m_ref, g_ref, wk_ref, wv_ref, k_ref, v_ref, kb_ref, vb_ref):
    mn = _rms(m_ref[...], g_ref[...]).astype(BF16)
    k = _dot(mn, wk_ref[...])
    v = _dot(mn, wv_ref[...])
    k_ref[...] = k
    v_ref[...] = v
    kb_ref[...] = k.astype(BF16)
    vb_ref[...] = v.astype(BF16)


def _memkv(mem, g, w_mk, w_mv):
    m = mem.shape[0]
    row = pl.BlockSpec((ROW_TILE, D_MODEL), lambda i: (i, 0))
    mat = _const_spec((D_MODEL, D_MODEL))
    return pl.pallas_call(
        _memkv_body,
        grid=(m // ROW_TILE,),
        in_specs=[row, _const_spec((1, D_MODEL)), mat, mat],
        out_specs=[row] * 4,
        out_shape=[jax.ShapeDtypeStruct((m, D_MODEL), F32)] * 2 + [jax.ShapeDtypeStruct((m, D_MODEL), BF16)] * 2,
        compiler_params=_params(1),
        name="memkv",
    )(mem, g, w_mk, w_mv)


def _rope_tables(pos):
    inv = ROPE_BASE ** (-jnp.arange(0, DK_RET, 2, dtype=F32) / DK_RET)
    ang = pos.astype(F32)[:, None] * inv[None, :]
    cos, sin = jnp.cos(ang), jnp.sin(ang)
    return jnp.concatenate([cos, cos], axis=1), jnp.concatenate([-sin, sin], axis=1)


def kernel(x_prompt, x_sample, mem_prompt, cache_ret_state, cache_sb_k, cache_sb_v, cache_mem_k, cache_mem_v,
           page_table, g_ffn1_pre, w_ffn1_gate, w_ffn1_up, w_ffn1_down, g_ffn1_post, g_mix_pre, w_in, b_sb,
           g_ret_gn, w_out, g_mix_post, g_mem, w_mk, w_mv, g_x_pre, w_xq, w_xo, g_x_post, g_ffn2_pre,
           w_ffn2_gate, w_ffn2_up, w_ffn2_down, g_ffn2_post):
    l = 0

    def wt(w):
        return w[l].astype(BF16)

    (g_ffn1_pre, g_ffn1_post, g_mix_pre, g_ret_gn, g_mix_post, g_mem, g_x_pre, g_x_post, g_ffn2_pre,
     g_ffn2_post) = (g[l][None, :] for g in (g_ffn1_pre, g_ffn1_post, g_mix_pre, g_ret_gn, g_mix_post, g_mem,
                                             g_x_pre, g_x_post, g_ffn2_pre, g_ffn2_post))
    ffn1 = (g_ffn1_pre, wt(w_ffn1_gate), wt(w_ffn1_up), wt(w_ffn1_down), g_ffn1_post)
    ffn2 = (g_ffn2_pre, wt(w_ffn2_gate), wt(w_ffn2_up), wt(w_ffn2_down), g_ffn2_post)
    w_in_b, w_out_b, w_xq_b, w_xo_b = wt(w_in), wt(w_out), wt(w_xq), wt(w_xo)
    lg = jnp.log1p(-jnp.exp2(-5.0 - jnp.arange(H_RET, dtype=F32)))
    bias = b_sb[l]
    cumsum_w = _cumsum_weights()

    xp = x_prompt.reshape(BATCH * SEQ, D_MODEL)
    h = _ffn(xp, *ffn1)
    cos_p, sin_p = _rope_tables(jnp.arange(SEQ))
    q_r, k_r, v_r, g_r, q_s, k_st, v_st, v_sb = _inproj(h, g_mix_pre, w_in_b, cos_p, sin_p,
                                                         (BF16, F32, BF16, F32, BF16), transposed=True)
    ret, rs_p = _ret_prompt(lg, q_r, k_r, v_r, g_r, g_ret_gn)
    o_s = _sb_prompt(bias, q_s, k_st, v_sb, cumsum_w)
    mk_p, mv_p, mk_b, mv_b = _memkv(mem_prompt.reshape(BATCH * N_MEM, D_MODEL), g_mem, wt(w_mk), wt(w_mv))
    h = _mix_xatt_prompt(h, ret, o_s, w_out_b, g_mix_post, g_x_pre, w_xq_b, mk_b, mv_b, w_xo_b, g_x_post)
    y_prompt = _ffn(h, *ffn2).reshape(BATCH, SEQ, D_MODEL)

    n_tok = DEC_BATCH * DEC_SEQ
    hs = _ffn(x_sample.reshape(n_tok, D_MODEL), *ffn1)
    cos_s, sin_s = _rope_tables(PAST_LEN + jnp.arange(DEC_SEQ))
    cos_s, sin_s = jnp.tile(cos_s, (DEC_BATCH, 1)), jnp.tile(sin_s, (DEC_BATCH, 1))
    sq_r, sk_r, sv_r, sg_r, sq_s, sk_s, sv_s = _inproj(hs, g_mix_pre, w_in_b, cos_s, sin_s, (F32,) * 5,
                                                       transposed=False)

    def per_seq(t):
        return t.reshape(DEC_BATCH, DEC_SEQ, t.shape[-1])

    s_ret, rs_s = _ret_sample(lg, per_seq(sq_r), per_seq(sk_r), per_seq(sv_r), per_seq(sg_r), g_ret_gn,
                              cache_ret_state[l])

    def pool_t(c):
        return c[l].transpose(0, 2, 3, 1)

    s_sb = _sb_sample(page_table.reshape(-1), bias, per_seq(sq_s), per_seq(sk_s), per_seq(sv_s), cumsum_w,
                      pool_t(cache_sb_k), pool_t(cache_sb_v))
    hs, sq_x = _mix_xq(hs, s_ret.reshape(n_tok, W_RET), s_sb.reshape(n_tok, W_SB), w_out_b, g_mix_post,
                       g_x_pre, w_xq_b, F32)
    def mem_rows(c):
        c = c[l].reshape(DEC_BATCH, N_MEM, H_X, 2, LANES)
        return c.transpose(0, 1, 3, 2, 4).reshape(DEC_BATCH, N_MEM, 2 * H_X, LANES)

    so = _xatt_sample(per_seq(sq_x), mem_rows(cache_mem_k), mem_rows(cache_mem_v))
    hs = _xo(hs, so.reshape(n_tok, D_MODEL), w_xo_b, g_x_post)
    y_sample = _ffn(hs, *ffn2).reshape(DEC_BATCH, DEC_SEQ, D_MODEL)

    def prompt_rows(t):
        return t.reshape(BATCH, H_SB, D_SB, SEQ).transpose(0, 3, 1, 2)[None]

    return (y_prompt, y_sample, rs_p[None],
            prompt_rows(k_st), prompt_rows(v_st),
            mk_p.reshape(1, BATCH, N_MEM, H_X, D_X), mv_p.reshape(1, BATCH, N_MEM, H_X, D_X),
            rs_s[None],
            sk_s.reshape(1, DEC_BATCH, DEC_SEQ, H_SB, D_SB), sv_s.reshape(1, DEC_BATCH, DEC_SEQ, H_SB, D_SB))
```

```python
import functools

import jax
import jax.numpy as jnp
import numpy as np
from jax import lax
from jax.experimental import pallas as pl
from jax.experimental.pallas import tpu as pltpu

F32 = jnp.float32
BF16 = jnp.bfloat16

D_MODEL = 1024
BATCH = 8
SEQ = 2048
DEC_BATCH = 128
DEC_SEQ = 4
PAST_LEN = 2048
PAGE_SIZE = 128
N_PAGES = PAST_LEN // PAGE_SIZE
W_RET = 512
DK_RET = 128
H_RET = 4
W_SB = 512
D_SB = 64
H_SB = 8
IN_COLS = 4 * W_RET + 3 * W_SB
N_MEM = 256
H_X = 4
D_X = 256
D_FF = 2816
CHUNK = 128
ROPE_BASE = 10000.0
EPS = 1e-6
LOG2E = 1.4426950408889634

LANES = 128
ROW_TILE = 512
FF_TILE = 256
SB_TILE = 256
SB_HEADS_PER_STEP = 4
RET_DEC_BLOCK = 8
XATT_DEC_BLOCK = 4
VMEM_LIMIT = 56 * 1024 * 1024

_NT = (((1,), (1,)), ((), ()))
_TN = (((0,), (0,)), ((), ()))


def _params(n_axes):
    return pltpu.CompilerParams(dimension_semantics=("arbitrary",) * n_axes, vmem_limit_bytes=VMEM_LIMIT)


def _const_spec(shape):
    zeros = (0,) * len(shape)
    return pl.BlockSpec(shape, lambda *_: zeros, pipeline_mode=pl.Buffered(1))


def _smem_spec():
    return pl.BlockSpec(memory_space=pltpu.SMEM)


def _rms(x, g):
    return x * lax.rsqrt(jnp.mean(x * x, axis=-1, keepdims=True) + EPS) * g


def _dot(a, b):
    return jnp.dot(a, b, preferred_element_type=F32)


def _silu(x):
    return x * jax.nn.sigmoid(x)


def _ffn_body(x_ref, gpre_ref, wg_ref, wu_ref, wd_ref, gpost_ref, o_ref, act_ref):
    x = x_ref[...]
    xn = _rms(x, gpre_ref[...]).astype(BF16)
    for c in range(D_FF // FF_TILE):
        sl = slice(c * FF_TILE, (c + 1) * FF_TILE)
        g = _dot(xn, wg_ref[:, sl])
        u = _dot(xn, wu_ref[:, sl])
        act_ref[:, sl] = (_silu(g) * u).astype(BF16)
    y = _dot(act_ref[...], wd_ref[...])
    o_ref[...] = x + 0.5 * _rms(y, gpost_ref[...])


def _ffn(x, g_pre, wg, wu, wd, g_post):
    m = x.shape[0]
    row = pl.BlockSpec((ROW_TILE, D_MODEL), lambda i: (i, 0))
    return pl.pallas_call(
        _ffn_body,
        grid=(m // ROW_TILE,),
        in_specs=[row, _const_spec((1, D_MODEL)), _const_spec((D_MODEL, D_FF)), _const_spec((D_MODEL, D_FF)),
                  _const_spec((D_FF, D_MODEL)), _const_spec((1, D_MODEL))],
        out_specs=row,
        out_shape=jax.ShapeDtypeStruct((m, D_MODEL), F32),
        scratch_shapes=[pltpu.VMEM((ROW_TILE, D_FF), BF16)],
        compiler_params=_params(1),
        name="ffn",
    )(x, g_pre, wg, wu, wd, g_post)


def _inproj_body(h_ref, g_ref, w_ref, cos_ref, sin_ref, qr_ref, kr_ref, vr_ref, gr_ref, qs_ref, ks_ref, vs_ref,
                 *vsb_ref, transposed):
    xn = _rms(h_ref[...], g_ref[...]).astype(BF16)
    cos2 = cos_ref[...]
    sin2 = sin_ref[...]

    def proj(k):
        return _dot(xn, w_ref[:, k * W_RET:(k + 1) * W_RET])

    def rope(p):
        outs = []
        for h in range(H_RET):
            x = p[:, h * DK_RET:(h + 1) * DK_RET]
            outs.append(x * cos2 + pltpu.roll(x, DK_RET // 2, 1) * sin2)
        return jnp.concatenate(outs, axis=1)

    qr_ref[...] = rope(proj(0)).astype(qr_ref.dtype)
    kr_ref[...] = (rope(proj(1)) * (DK_RET ** -0.5)).astype(kr_ref.dtype)
    vr_ref[...] = proj(2).astype(vr_ref.dtype)
    gr_ref[...] = proj(3).astype(gr_ref.dtype)
    qs_ref[...] = (proj(4) * (D_SB ** -0.5 * LOG2E)).astype(qs_ref.dtype)
    ks, vs = proj(5), proj(6)
    if transposed:
        ks_ref[0] = ks.T
        vs_ref[0] = vs.T
        vsb_ref[0][...] = vs.astype(BF16)
    else:
        ks_ref[...] = ks
        vs_ref[...] = vs


def _inproj(h, g, w, cos2, sin2, dtypes, transposed):
    m = h.shape[0]
    n_tab = cos2.shape[0] // ROW_TILE
    row = pl.BlockSpec((ROW_TILE, D_MODEL), lambda i: (i, 0))
    tab = pl.BlockSpec((ROW_TILE, LANES), lambda i: (i % n_tab, 0))
    out = pl.BlockSpec((ROW_TILE, W_RET), lambda i: (i, 0))
    outs = [out] * 5
    shapes = [jax.ShapeDtypeStruct((m, W_RET), dt) for dt in dtypes]
    if transposed:
        tiles = SEQ // ROW_TILE
        out_t = pl.BlockSpec((1, W_SB, ROW_TILE), lambda i: (i // tiles, 0, i % tiles))
        outs += [out_t, out_t, out]
        shapes += [jax.ShapeDtypeStruct((m // SEQ, W_SB, SEQ), F32)] * 2 + [jax.ShapeDtypeStruct((m, W_SB), BF16)]
    else:
        outs += [out, out]
        shapes += [jax.ShapeDtypeStruct((m, W_SB), F32)] * 2
    return pl.pallas_call(
        functools.partial(_inproj_body, transposed=transposed),
        grid=(m // ROW_TILE,),
        in_specs=[row, _const_spec((1, D_MODEL)), _const_spec((D_MODEL, IN_COLS)), tab, tab],
        out_specs=outs,
        out_shape=shapes,
        compiler_params=_params(1),
        name="inproj",
    )(h, g, w, cos2, sin2)


def _ret_decays(lg, length):
    i = lax.broadcasted_iota(jnp.int32, (CHUNK, CHUNK), 0).astype(F32)
    j = lax.broadcasted_iota(jnp.int32, (CHUNK, CHUNK), 1).astype(F32)
    diff = i - j
    causal = diff >= 0
    dmat = jnp.where(causal, jnp.exp(jnp.where(causal, diff, 0.0) * lg), 0.0)
    rowdec = jnp.exp((i + 1.0) * lg)
    kdec = jnp.exp((length - 1.0 - i) * lg)
    sdec = jnp.exp(jnp.full((CHUNK, CHUNK), float(length), F32) * lg)
    return dmat, rowdec, kdec, sdec


def _ret_chunk_heads(qkv, states, decays):
    first = []
    for (q, k, v), s, (_, _, kdec, _) in zip(qkv, states, decays):
        scores = lax.dot_general(q, k.astype(BF16), _NT, preferred_element_type=F32)
        cross = _dot(q, s.astype(BF16))
        update = lax.dot_general((k * kdec).astype(BF16), v, _TN, preferred_element_type=F32)
        first.append((scores, cross, update))
    outs = []
    for (q, k, v), s, (dmat, rowdec, _, sdec), (scores, cross, update) in zip(qkv, states, decays, first):
        inner = _dot((scores * dmat).astype(BF16), v)
        outs.append((inner + cross * rowdec, sdec * s + update))
    return outs


def _ret_gate(o, gr, ggn):
    mu = jnp.mean(o, axis=-1, keepdims=True)
    d = o - mu
    var = jnp.mean(d * d, axis=-1, keepdims=True)
    return _silu(gr) * (d * lax.rsqrt(var + EPS) * ggn)


def _ret_prompt_body(lg_ref, q_ref, k_ref, v_ref, gr_ref, ggn_ref, o_ref, s_ref):
    heads = [slice(h * DK_RET, (h + 1) * DK_RET) for h in range(H_RET)]
    decays = [_ret_decays(lg_ref[h], CHUNK) for h in range(H_RET)]

    def chunk_pair(c2, states):
        rows2 = [pl.ds(pl.multiple_of((2 * c2 + i) * CHUNK, CHUNK), CHUNK) for i in range(2)]
        qkv2 = [[(q_ref[rows, cols], k_ref[rows, cols], v_ref[rows, cols]) for cols in heads] for rows in rows2]
        first2 = [[(lax.dot_general(q, k.astype(BF16), _NT, preferred_element_type=F32),
                    lax.dot_general((k * kdec).astype(BF16), v, _TN, preferred_element_type=F32))
                   for (q, k, v), (_, _, kdec, _) in zip(qkv, decays)] for qkv in qkv2]
        states = list(states)
        cross2 = []
        for qkv, first in zip(qkv2, first2):
            cross2.append([_dot(q, s.astype(BF16)) for (q, _, _), s in zip(qkv, states)])
            states = [sdec * s + update for s, (_, update), (_, _, _, sdec) in zip(states, first, decays)]
        for rows, qkv, first, cross in zip(rows2, qkv2, first2, cross2):
            for cols, (_, _, v), (scores, _), cr, (dmat, rowdec, _, _) in zip(heads, qkv, first, cross, decays):
                o = _dot((scores * dmat).astype(BF16), v) + cr * rowdec
                o_ref[rows, cols] = _ret_gate(o, gr_ref[rows, cols], ggn_ref[:, cols]).astype(o_ref.dtype)
        return tuple(states)

    zero = jnp.zeros((CHUNK, CHUNK), F32)
    states = lax.fori_loop(0, SEQ // CHUNK // 2, chunk_pair, (zero,) * H_RET)
    for h in range(H_RET):
        s_ref[0, h] = states[h]


def _ret_prompt(lg, q, k, v, gr, ggn):
    blk = pl.BlockSpec((SEQ, W_RET), lambda b: (b, 0))
    return pl.pallas_call(
        _ret_prompt_body,
        grid=(BATCH,),
        in_specs=[_smem_spec(), blk, blk, blk, blk, _const_spec((1, W_RET))],
        out_specs=[blk, pl.BlockSpec((1, H_RET, DK_RET, DK_RET), lambda b: (b, 0, 0, 0))],
        out_shape=[jax.ShapeDtypeStruct((BATCH * SEQ, W_RET), BF16),
                   jax.ShapeDtypeStruct((BATCH, H_RET, DK_RET, DK_RET), F32)],
        compiler_params=_params(1),
        name="ret_prompt",
    )(lg, q, k, v, gr, ggn)


def _ret_sample_body(lg_ref, q_ref, k_ref, v_ref, gr_ref, ggn_ref, s0_ref, o_ref, s_ref):
    pad = jnp.zeros((CHUNK - DEC_SEQ, DK_RET), F32)
    heads = [slice(h * DK_RET, (h + 1) * DK_RET) for h in range(H_RET)]
    decays = [_ret_decays(lg_ref[h], DEC_SEQ) for h in range(H_RET)]
    for b in range(RET_DEC_BLOCK):
        def tile(ref, cols):
            return jnp.concatenate([ref[b, :, cols], pad], axis=0)

        qkv = [(tile(q_ref, cols).astype(BF16), tile(k_ref, cols), tile(v_ref, cols).astype(BF16)) for cols in heads]
        states = [s0_ref[b, h] for h in range(H_RET)]
        for h, (o, s_new) in enumerate(_ret_chunk_heads(qkv, states, decays)):
            s_ref[b, h] = s_new
            o_ref[b, :, heads[h]] = _ret_gate(o, tile(gr_ref, heads[h]), ggn_ref[:, heads[h]])[:DEC_SEQ]


def _ret_sample(lg, q, k, v, gr, ggn, s0):
    tok = pl.BlockSpec((RET_DEC_BLOCK, DEC_SEQ, W_RET), lambda i: (i, 0, 0))
    st = pl.BlockSpec((RET_DEC_BLOCK, H_RET, DK_RET, DK_RET), lambda i: (i, 0, 0, 0))
    return pl.pallas_call(
        _ret_sample_body,
        grid=(DEC_BATCH // RET_DEC_BLOCK,),
        in_specs=[_smem_spec(), tok, tok, tok, tok, _const_spec((1, W_RET)), st],
        out_specs=[tok, st],
        out_shape=[jax.ShapeDtypeStruct((DEC_BATCH, DEC_SEQ, W_RET), F32),
                   jax.ShapeDtypeStruct((DEC_BATCH, H_RET, DK_RET, DK_RET), F32)],
        compiler_params=_params(1),
        name="ret_sample",
    )(lg, q, k, v, gr, ggn, s0)


def _cumsum_weights():
    s_src = np.arange(LANES)[:, None]
    s_dst = np.arange(LANES)[None, :]
    half = np.concatenate([(s_src > s_dst).astype(np.float32), np.ones((LANES, LANES), np.float32)], axis=1)
    return jnp.asarray(np.concatenate([half, half], axis=0), dtype=BF16)


def _softplus2(z2):
    return jnp.maximum(z2, 0.0) + jnp.log2(1.0 + jnp.exp2(jnp.minimum(z2, -z2)))


def _split_bf16(x):
    hi = x.astype(BF16)
    lo = (x - hi.astype(F32)).astype(BF16)
    return hi, lo


def _sb_prompt_body(bias_ref, q_ref, kt_ref, v_ref, w_ref, o_ref):
    hp = pl.program_id(1)
    t = SB_TILE
    lane = lax.broadcasted_iota(jnp.int32, (t, LANES), 1)
    row = lax.broadcasted_iota(jnp.int32, (t, t), 0)
    col = lax.broadcasted_iota(jnp.int32, (t, t), 1)
    strictly_earlier = col < row
    w = w_ref[...]

    def log_weights(z, diagonal):
        sp = _softplus2(z)
        hi, lo = _split_bf16(jnp.where(strictly_earlier, sp, 0.0) if diagonal else sp)
        halves = []
        for half in (slice(0, LANES), slice(LANES, 2 * LANES)):
            r = _dot(jnp.concatenate([hi[:, half], lo[:, half]], axis=1), w)
            halves.append((z[:, half] - sp[:, half] - r[:, :LANES], r[:, LANES:]))
        return halves

    def attend(halves, vb, carry, diagonal):
        o, c = carry
        (logw1, total1), (logw2, total2) = halves
        a2 = jnp.exp2(logw2 - c)
        c = c + total2
        a1 = jnp.exp2(logw1 - c)
        c = c + total1
        a = jnp.concatenate([a1, a2], axis=1)
        if diagonal:
            a = jnp.where(strictly_earlier, a, 0.0)
        return o + _dot(a.astype(BF16), vb), c

    def q_tile(qi, _):
        rows = pl.ds(pl.multiple_of(qi * t, t), t)
        qms, biases = [], []
        for hh in range(SB_HEADS_PER_STEP):
            qt = q_ref[rows, (hh // 2) * LANES:(hh // 2 + 1) * LANES]
            own = (lane < D_SB) if hh % 2 == 0 else (lane >= D_SB)
            qms.append(jnp.where(own, qt, jnp.zeros_like(qt)))
            biases.append(bias_ref[SB_HEADS_PER_STEP * hp + hh] * LOG2E)

        def key_tiles(kjs, carries, diagonal=False):
            kvs = []
            for kj in kjs:
                krows = pl.ds(pl.multiple_of(kj * t, t), t)
                kvs.append([(kt_ref[0, g * LANES:(g + 1) * LANES, krows].astype(BF16),
                             v_ref[krows, g * LANES:(g + 1) * LANES]) for g in range(SB_HEADS_PER_STEP // 2)])
            zs = [[_dot(qms[hh], kv[hh // 2][0]) + biases[hh] for hh in range(SB_HEADS_PER_STEP)] for kv in kvs]
            parts = [[log_weights(z, diagonal) for z in zh] for zh in zs]
            carries = list(carries)
            for kv, ph in zip(kvs, parts):
                for hh in range(SB_HEADS_PER_STEP):
                    carries[hh] = attend(ph[hh], kv[hh // 2][1], carries[hh], diagonal)
            return tuple(carries)

        zero = jnp.zeros((t, LANES), F32)
        carries = key_tiles((qi,), ((zero, zero),) * SB_HEADS_PER_STEP, diagonal=True)
        carries = lax.cond(qi % 2 == 1, lambda cr: key_tiles((qi - 1,), cr), lambda cr: cr, carries)
        top = qi - 1 - qi % 2

        def key_tile_pair(n, carries):
            return key_tiles((top - 2 * n, top - 2 * n - 1), carries)

        carries = lax.fori_loop(0, qi // 2, key_tile_pair, carries)
        for g in range(SB_HEADS_PER_STEP // 2):
            o_ref[rows, g * LANES:(g + 1) * LANES] = jnp.where(
                lane < D_SB, carries[2 * g][0], carries[2 * g + 1][0]).astype(o_ref.dtype)
        return 0

    lax.fori_loop(0, SEQ // t, q_tile, 0)


def _sb_prompt(bias, q, k_t, v, w):
    width = SB_HEADS_PER_STEP * D_SB
    blk = pl.BlockSpec((SEQ, width), lambda b, hp: (b, hp))
    blk_t = pl.BlockSpec((1, width, SEQ), lambda b, hp: (b, hp, 0))
    return pl.pallas_call(
        _sb_prompt_body,
        grid=(BATCH, W_SB // width),
        in_specs=[_smem_spec(), blk, blk_t, blk, _const_spec((2 * LANES, 2 * LANES))],
        out_specs=blk,
        out_shape=jax.ShapeDtypeStruct((BATCH * SEQ, W_SB), BF16),
        compiler_params=_params(2),
        name="sb_prompt",
    )(bias, q, k_t, v, w)


def _sb_sample_body(pt_ref, bias_ref, q_ref, kn_ref, vn_ref, w_ref, *rest):
    kps = rest[:N_PAGES]
    vps = rest[N_PAGES:2 * N_PAGES]
    o_ref = rest[2 * N_PAGES]
    rows = H_SB * 8
    head_r = lax.broadcasted_iota(jnp.int32, (rows, W_SB), 0) // 8
    head_c = lax.broadcasted_iota(jnp.int32, (rows, W_SB), 1) // D_SB
    own = head_r == head_c
    q8 = jnp.concatenate([q_ref[0], jnp.zeros((8 - DEC_SEQ, W_SB), F32)], axis=0)
    qbd = jnp.where(own, jnp.concatenate([q8] * H_SB, axis=0), 0.0).astype(BF16)
    row128 = lax.broadcasted_iota(jnp.int32, (rows, LANES), 0)
    pos128 = lax.broadcasted_iota(jnp.int32, (rows, LANES), 1)
    bias = jnp.zeros((rows, LANES), F32)
    for h in range(H_SB):
        bias = jnp.where(row128 // 8 == h, bias_ref[h] * LOG2E, bias)
    new_mask = pos128 < (row128 % 8)
    w = w_ref[...]

    def log_weights(z, mask):
        z = z + bias
        sp = _softplus2(z)
        hi, lo = _split_bf16(sp if mask is None else jnp.where(mask, sp, 0.0))
        r = _dot(jnp.concatenate([hi, lo], axis=1), w)
        return z - sp - r[:, :LANES], r[:, LANES:]

    pad = jnp.zeros((PAGE_SIZE - DEC_SEQ, W_SB), F32)
    kn = jnp.concatenate([kn_ref[0], pad], axis=0).astype(BF16)
    vn = jnp.concatenate([vn_ref[0], pad], axis=0).astype(BF16)
    zs = [lax.dot_general(qbd, kn, _NT, preferred_element_type=F32)]
    zs += [_dot(qbd, kps[s][0].reshape(W_SB, PAGE_SIZE).astype(BF16)) for s in range(N_PAGES)]
    parts = [log_weights(z, new_mask if i == 0 else None) for i, z in enumerate(zs)]
    logw, c = parts[0]
    acc = _dot(jnp.where(new_mask, jnp.exp2(logw), 0.0).astype(BF16), vn)
    for s in range(N_PAGES):
        logw, total = parts[s + 1]
        a = jnp.exp2(logw - c)
        vt = vps[s][0].reshape(W_SB, PAGE_SIZE).astype(BF16)
        acc = acc + lax.dot_general(a.astype(BF16), vt, _NT, preferred_element_type=F32)
        c = c + total
    diag = jnp.where(own, acc, 0.0).reshape(H_SB, 8, W_SB)
    o_ref[0] = jnp.sum(diag, axis=0)[:DEC_SEQ]


def _sb_sample(pt, bias, q, k_new, v_new, w, kc_t, vc_t):
    tok = pl.BlockSpec((1, DEC_SEQ, W_SB), lambda b, pt: (b, 0, 0))

    def page_spec(s):
        return pl.BlockSpec((1, H_SB, D_SB, PAGE_SIZE), lambda b, pt: (pt[b * N_PAGES + (N_PAGES - 1 - s)], 0, 0, 0))

    pages = [page_spec(s) for s in range(N_PAGES)]
    grid_spec = pltpu.PrefetchScalarGridSpec(
        num_scalar_prefetch=1,
        grid=(DEC_BATCH,),
        in_specs=[_smem_spec(), tok, tok, tok, pl.BlockSpec((2 * LANES, 2 * LANES), lambda b, pt: (0, 0))]
        + pages + pages,
        out_specs=tok,
    )
    return pl.pallas_call(
        _sb_sample_body,
        grid_spec=grid_spec,
        out_shape=jax.ShapeDtypeStruct((DEC_BATCH, DEC_SEQ, W_SB), F32),
        compiler_params=_params(1),
        name="sb_sample",
    )(pt, bias, q, k_new, v_new, w, *([kc_t] * N_PAGES), *([vc_t] * N_PAGES))


def _mix_xq_body(h_ref, r_ref, s_ref, wout_ref, gpost_ref, gx_ref, wxq_ref, h2_ref, qx_ref):
    cat = jnp.concatenate([r_ref[...].astype(BF16), s_ref[...].astype(BF16)], axis=1)
    h2 = h_ref[...] + _rms(_dot(cat, wout_ref[...]), gpost_ref[...])
    h2_ref[...] = h2
    xn = _rms(h2, gx_ref[...]).astype(BF16)
    qx_ref[...] = (_dot(xn, wxq_ref[...]) * (D_X ** -0.5)).astype(qx_ref.dtype)


def _mix_xq(h, ret, sb, w_out, g_post, g_x, w_xq, q_dtype):
    m = h.shape[0]
    row = pl.BlockSpec((ROW_TILE, D_MODEL), lambda i: (i, 0))
    half = pl.BlockSpec((ROW_TILE, W_RET), lambda i: (i, 0))
    vec = _const_spec((1, D_MODEL))
    mat = _const_spec((D_MODEL, D_MODEL))
    return pl.pallas_call(
        _mix_xq_body,
        grid=(m // ROW_TILE,),
        in_specs=[row, half, half, mat, vec, vec, mat],
        out_specs=[row, row],
        out_shape=[jax.ShapeDtypeStruct((m, D_MODEL), F32), jax.ShapeDtypeStruct((m, D_MODEL), q_dtype)],
        compiler_params=_params(1),
        name="mix_xq",
    )(h, ret, sb, w_out, g_post, g_x, w_xq)


def _softmax_rows(s):
    e = jnp.exp(s - jnp.max(s, axis=-1, keepdims=True))
    return e / jnp.sum(e, axis=-1, keepdims=True)


def _mix_xatt_prompt_body(h_ref, r_ref, s_ref, wout_ref, gpost_ref, gx_ref, wxq_ref, mk_ref, mv_ref, wxo_ref,
                          gxpost_ref, out_ref):
    heads = [slice(hx * D_X, (hx + 1) * D_X) for hx in range(H_X)]
    halves = [slice(i * (ROW_TILE // 2), (i + 1) * (ROW_TILE // 2)) for i in range(2)]
    mixed = [_dot(jnp.concatenate([r_ref[rows, :], s_ref[rows, :]], axis=1), wout_ref[...]) for rows in halves]
    h2 = [h_ref[rows, :] + _rms(y, gpost_ref[...]) for rows, y in zip(halves, mixed)]
    q = [(_dot(_rms(x, gx_ref[...]).astype(BF16), wxq_ref[...]) * (D_X ** -0.5)).astype(BF16)
         for x in h2]
    scores = [[lax.dot_general(qh[:, cols], mk_ref[:, cols], _NT, preferred_element_type=F32) for cols in heads]
              for qh in q]
    o = [jnp.concatenate([_dot(_softmax_rows(s).astype(BF16), mv_ref[:, cols]).astype(BF16)
                          for s, cols in zip(sh, heads)], axis=1) for sh in scores]
    y = [_dot(oh, wxo_ref[...]) for oh in o]
    for rows, x, yh in zip(halves, h2, y):
        out_ref[rows, :] = x + _rms(yh, gxpost_ref[...])


def _mix_xatt_prompt(h, ret, sb, w_out, g_post, g_x, w_xq, mk, mv, w_xo, g_xpost):
    m = h.shape[0]
    tiles_per_batch = SEQ // ROW_TILE
    row = pl.BlockSpec((ROW_TILE, D_MODEL), lambda i: (i, 0))
    half = pl.BlockSpec((ROW_TILE, W_RET), lambda i: (i, 0))
    mem = pl.BlockSpec((N_MEM, D_MODEL), lambda i: (i // tiles_per_batch, 0))
    vec = _const_spec((1, D_MODEL))
    mat = _const_spec((D_MODEL, D_MODEL))
    return pl.pallas_call(
        _mix_xatt_prompt_body,
        grid=(m // ROW_TILE,),
        in_specs=[row, half, half, mat, vec, vec, mat, mem, mem, mat, vec],
        out_specs=row,
        out_shape=jax.ShapeDtypeStruct((m, D_MODEL), F32),
        compiler_params=_params(1),
        name="mix_xatt_prompt",
    )(h, ret, sb, w_out, g_post, g_x, w_xq, mk, mv, w_xo, g_xpost)


def _xatt_sample_body(q_ref, mk_ref, mv_ref, o_ref):
    n_rows = 2 * H_X * 8
    n_keys = N_MEM * 8
    row = lax.broadcasted_iota(jnp.int32, (n_rows, n_keys), 0)
    col = lax.broadcasted_iota(jnp.int32, (n_rows, n_keys), 1)
    own = ((col // H_X) % 2 == row // (H_X * 8)) & (col % H_X == (row // 8) % H_X)
    row_h = lax.broadcasted_iota(jnp.int32, (H_X * 8, n_keys), 0)
    col_h = lax.broadcasted_iota(jnp.int32, (H_X * 8, n_keys), 1)
    summed = (col_h % 8) == H_X + row_h // 8
    pad = jnp.zeros((8 - DEC_SEQ, D_MODEL), F32)

    def rotate_lanes(x, shift):
        return jnp.concatenate([pltpu.roll(x[:, g * LANES:(g + 1) * LANES], shift, 1)
                                for g in range(x.shape[1] // LANES)], axis=1)

    def stacked_queries(b):
        q8 = jnp.concatenate([q_ref[b], pad], axis=0)
        blocks = [q8[:, (hx * 2 + half) * LANES:(hx * 2 + half + 1) * LANES] for half in range(2) for hx in range(H_X)]
        return jnp.concatenate(blocks, axis=0).astype(BF16)

    scores = [lax.dot_general(stacked_queries(b), mk_ref[b].reshape(n_keys, LANES).astype(BF16), _NT,
                              preferred_element_type=F32) for b in range(XATT_DEC_BLOCK)]
    for b, s in enumerate(scores):
        part = jnp.where(own, s, 0.0)
        part = part[:H_X * 8] + part[H_X * 8:]
        full = jnp.where(summed, part + rotate_lanes(part, H_X), -jnp.inf)
        p = _softmax_rows(full)
        p2 = jnp.concatenate([rotate_lanes(p, LANES - H_X), p], axis=0).astype(BF16)
        o = _dot(p2, mv_ref[b].reshape(n_keys, LANES).astype(BF16))
        blocks = [o[(half * H_X + hx) * 8:(half * H_X + hx + 1) * 8] for hx in range(H_X) for half in range(2)]
        o_ref[b] = jnp.concatenate(blocks, axis=1)[:DEC_SEQ]


def _xatt_sample(q, mk, mv):
    tok = pl.BlockSpec((XATT_DEC_BLOCK, DEC_SEQ, D_MODEL), lambda i: (i, 0, 0))
    mem = pl.BlockSpec((XATT_DEC_BLOCK, N_MEM, 8, LANES), lambda i: (i, 0, 0, 0))
    return pl.pallas_call(
        _xatt_sample_body,
        grid=(DEC_BATCH // XATT_DEC_BLOCK,),
        in_specs=[tok, mem, mem],
        out_specs=tok,
        out_shape=jax.ShapeDtypeStruct((DEC_BATCH, DEC_SEQ, D_MODEL), F32),
        compiler_params=_params(1),
        name="xatt_sample",
    )(q, mk, mv)


def _xo_body(h_ref, o_ref, wxo_ref, g_ref, out_ref):
    out_ref[...] = h_ref[...] + _rms(_dot(o_ref[...].astype(BF16), wxo_ref[...]), g_ref[...])


def _xo(h, o, w_xo, g):
    m = h.shape[0]
    row = pl.BlockSpec((ROW_TILE, D_MODEL), lambda i: (i, 0))
    return pl.pallas_call(
        _xo_body,
        grid=(m // ROW_TILE,),
        in_specs=[row, row, _const_spec((D_MODEL, D_MODEL)), _const_spec((1, D_MODEL))],
        out_specs=row,
        out_shape=jax.ShapeDtypeStruct((m, D_MODEL), F32),
        compiler_params=_params(1),
        name="xo",
    )(h, o, w_xo, g)


def _memkv_body(m_ref, g_ref, wk_ref, wv_ref, k_ref, v_ref, kb_ref, vb_ref):
    mn = _rms(m_ref[...], g_ref[...]).astype(BF16)
    k = _dot(mn, wk_ref[...])
    v = _dot(mn, wv_ref[...])
    k_ref[...] = k
    v_ref[...] = v
    kb_ref[...] = k.astype(BF16)
    vb_ref[...] = v.astype(BF16)


def _memkv(mem, g, w_mk, w_mv):
    m = mem.shape[0]
    row = pl.BlockSpec((ROW_TILE, D_MODEL), lambda i: (i, 0))
    mat = _const_spec((D_MODEL, D_MODEL))
    return pl.pallas_call(
        _memkv_body,
        grid=(m // ROW_TILE,),
        in_specs=[row, _const_spec((1, D_MODEL)), mat, mat],
        out_specs=[row] * 4,
        out_shape=[jax.ShapeDtypeStruct((m, D_MODEL), F32)] * 2 + [jax.ShapeDtypeStruct((m, D_MODEL), BF16)] * 2,
        compiler_params=_params(1),
        name="memkv",
    )(mem, g, w_mk, w_mv)


def _rope_tables(pos):
    inv = ROPE_BASE ** (-jnp.arange(0, DK_RET, 2, dtype=F32) / DK_RET)
    ang = pos.astype(F32)[:, None] * inv[None, :]
    cos, sin = jnp.cos(ang), jnp.sin(ang)
    return jnp.concatenate([cos, cos], axis=1), jnp.concatenate([-sin, sin], axis=1)


def kernel(x_prompt, x_sample, mem_prompt, cache_ret_state, cache_sb_k, cache_sb_v, cache_mem_k, cache_mem_v,
           page_table, g_ffn1_pre, w_ffn1_gate, w_ffn1_up, w_ffn1_down, g_ffn1_post, g_mix_pre, w_in, b_sb,
           g_ret_gn, w_out, g_mix_post, g_mem, w_mk, w_mv, g_x_pre, w_xq, w_xo, g_x_post, g_ffn2_pre,
           w_ffn2_gate, w_ffn2_up, w_ffn2_down, g_ffn2_post):
    l = 0

    def wt(w):
        return w[l].astype(BF16)

    (g_ffn1_pre, g_ffn1_post, g_mix_pre, g_ret_gn, g_mix_post, g_mem, g_x_pre, g_x_post, g_ffn2_pre,
     g_ffn2_post) = (g[l][None, :] for g in (g_ffn1_pre, g_ffn1_post, g_mix_pre, g_ret_gn, g_mix_post, g_mem,
                                             g_x_pre, g_x_post, g_ffn2_pre, g_ffn2_post))
    ffn1 = (g_ffn1_pre, wt(w_ffn1_gate), wt(w_ffn1_up), wt(w_ffn1_down), g_ffn1_post)
    ffn2 = (g_ffn2_pre, wt(w_ffn2_gate), wt(w_ffn2_up), wt(w_ffn2_down), g_ffn2_post)
    w_in_b, w_out_b, w_xq_b, w_xo_b = wt(w_in), wt(w_out), wt(w_xq), wt(w_xo)
    lg = jnp.log1p(-jnp.exp2(-5.0 - jnp.arange(H_RET, dtype=F32)))
    bias = b_sb[l]
    cumsum_w = _cumsum_weights()

    xp = x_prompt.reshape(BATCH * SEQ, D_MODEL)
    h = _ffn(xp, *ffn1)
    cos_p, sin_p = _rope_tables(jnp.arange(SEQ))
    q_r, k_r, v_r, g_r, q_s, k_st, v_st, v_sb = _inproj(h, g_mix_pre, w_in_b, cos_p, sin_p,
                                                         (BF16, F32, BF16, F32, BF16), transposed=True)
    ret, rs_p = _ret_prompt(lg, q_r, k_r, v_r, g_r, g_ret_gn)
    o_s = _sb_prompt(bias, q_s, k_st, v_sb, cumsum_w)
    mk_p, mv_p, mk_b, mv_b = _memkv(mem_prompt.reshape(BATCH * N_MEM, D_MODEL), g_mem, wt(w_mk), wt(w_mv))
    h = _mix_xatt_prompt(h, ret, o_s, w_out_b, g_mix_post, g_x_pre, w_xq_b, mk_b, mv_b, w_xo_b, g_x_post)
    y_prompt = _ffn(h, *ffn2).reshape(BATCH, SEQ, D_MODEL)

    n_tok = DEC_BATCH * DEC_SEQ
    hs = _ffn(x_sample.reshape(n_tok, D_MODEL), *ffn1)
    cos_s, sin_s = _rope_tables(PAST_LEN + jnp.arange(DEC_SEQ))
    cos_s, sin_s = jnp.tile(cos_s, (DEC_BATCH, 1)), jnp.tile(sin_s, (DEC_BATCH, 1))
    sq_r, sk_r, sv_r, sg_r, sq_s, sk_s, sv_s = _inproj(hs, g_mix_pre, w_in_b, cos_s, sin_s, (F32,) * 5,
                                                       transposed=False)

    def per_seq(t):
        return t.reshape(DEC_BATCH, DEC_SEQ, t.shape[-1])

    s_ret, rs_s = _ret_sample(lg, per_seq(sq_r), per_seq(sk_r), per_seq(sv_r), per_seq(sg_r), g_ret_gn,
                              cache_ret_state[l])

    def pool_t(c):
        return c[l].transpose(0, 2, 3, 1)

    s_sb = _sb_sample(page_table.reshape(-1), bias, per_seq(sq_s), per_seq(sk_s), per_seq(sv_s), cumsum_w,
                      pool_t(cache_sb_k), pool_t(cache_sb_v))
    hs, sq_x = _mix_xq(hs, s_ret.reshape(n_tok, W_RET), s_sb.reshape(n_tok, W_SB), w_out_b, g_mix_post,
                       g_x_pre, w_xq_b, F32)
    def mem_rows(c):
        c = c[l].reshape(DEC_BATCH, N_MEM, H_X, 2, LANES)
        return c.transpose(0, 1, 3, 2, 4).reshape(DEC_BATCH, N_MEM, 2 * H_X, LANES)

    so = _xatt_sample(per_seq(sq_x), mem_rows(cache_mem_k), mem_rows(cache_mem_v))
    hs = _xo(hs, so.reshape(n_tok, D_MODEL), w_xo_b, g_x_post)
    y_sample = _ffn(hs, *ffn2).reshape(DEC_BATCH, DEC_SEQ, D_MODEL)

    def prompt_rows(t):
        return t.reshape(BATCH, H_SB, D_SB, SEQ).transpose(0, 3, 1, 2)[None]

    return (y_prompt, y_sample, rs_p[None],
            prompt_rows(k_st), prompt_rows(v_st),
            mk_p.reshape(1, BATCH, N_MEM, H_X, D_X), mv_p.reshape(1, BATCH, N_MEM, H_X, D_X),
            rs_s[None],
            sk_s.reshape(1, DEC_BATCH, DEC_SEQ, H_SB, D_SB), sv_s.reshape(1, DEC_BATCH, DEC_SEQ, H_SB, D_SB))
```
